```python
import math
import jax, jax.numpy as jnp
from jax import lax
import numpy as np

D_MODEL = 4096
BATCH = 4
SEQ = 2048
DEPTH = 2
DEC_BATCH = 8
DEC_SEQ = 8
PAST_LEN = 16384
PAGE_SIZE = 128

HEAD_DIM = 128
N_EVEN = (DEPTH + 1) // 2
N_ODD = DEPTH // 2
D_CONV = D_MODEL // 2
CONV_W = 3
NSA_HEADS = D_MODEL // 2 // HEAD_DIM
NSA_KV_HEADS = NSA_HEADS // 4
NSA_GROUP = NSA_HEADS // NSA_KV_HEADS
N_BRANCH = 3
CMP_BLOCK = 64
TOP_N = 16
WINDOW = 512
NSA_QBLOCK = 16
FORCED_SCORE = float(NSA_GROUP + 1)
D_Q_NSA = NSA_HEADS * HEAD_DIM
D_KV_NSA = NSA_KV_HEADS * HEAD_DIM
D_IN_EVEN = 3 * D_CONV + D_Q_NSA + 2 * N_BRANCH * D_KV_NSA + N_BRANCH * NSA_HEADS
SPLIT_EVEN = [D_CONV, 2 * D_CONV, 3 * D_CONV, 3 * D_CONV + D_Q_NSA, 3 * D_CONV + D_Q_NSA + 2 * N_BRANCH * D_KV_NSA]
D_MIX_EVEN = D_CONV + D_Q_NSA
SB_HEADS = D_MODEL // HEAD_DIM
D_SB = SB_HEADS * HEAD_DIM
SB_QBLOCK = 128
N_BUCKETS = 32
MAX_DISTANCE = 128
D_FF = 256 * ((8 * D_MODEL // 3 + 255) // 256)
FFN_CONV_W = 3
RMS_EPS = 1e-6

kernel_name = 'hybrid_conv_nsa_stickbreaking_decode_step'


def rmsnorm(x, g):
    xf = x.astype(jnp.float32)
    y = xf * lax.rsqrt(jnp.mean(xf * xf, axis=-1, keepdims=True) + RMS_EPS)
    return (y * g.astype(jnp.float32)).astype(x.dtype)


def causal_dwconv(u, hist, w):
    width = w.shape[0]
    T = u.shape[1]
    up = jnp.concatenate([hist.astype(u.dtype), u], axis=1)
    y = w[width - 1] * u
    for i in range(width - 1):
        y = y + w[i] * up[:, i:i + T]
    return y, up[:, T:]


def masked_softmax(logits, mask):
    l = jnp.where(mask, logits.astype(jnp.float32), -1e30)
    m = jnp.max(l, axis=-1, keepdims=True)
    e = jnp.where(mask, jnp.exp(l - m), 0.0)
    return e / jnp.maximum(jnp.sum(e, axis=-1, keepdims=True), 1e-30)


def t5_bucket(dist):
    n = jnp.maximum(dist, 0)
    max_exact = N_BUCKETS // 2
    nf = jnp.maximum(n, 1).astype(jnp.float32)
    large = max_exact + (jnp.log(nf / max_exact) / math.log(MAX_DISTANCE / max_exact) * (N_BUCKETS - max_exact)).astype(jnp.int32)
    large = jnp.minimum(large, N_BUCKETS - 1)
    return jnp.where(n < max_exact, n, large)


def head_bias(rel_bias, dist):
    b = jnp.take(rel_bias, t5_bucket(dist), axis=0)
    qb, nk = dist.shape
    return b.reshape(qb, nk, NSA_KV_HEADS, NSA_GROUP).transpose(2, 3, 0, 1).astype(jnp.float32)


def gather_pages(pool, page_table):
    g = pool[page_table]
    return g.reshape(g.shape[0], g.shape[1] * g.shape[2], g.shape[3], g.shape[4])


def compress_blocks(k, w_cmp, pe, n_blocks):
    b, _, g, hd = k.shape
    blocks = k[:, :n_blocks * CMP_BLOCK].reshape(b, n_blocks, CMP_BLOCK, g, hd) + pe[None, None, :, None, :]
    return jnp.einsum('bnlgd,lde->bnge', blocks, w_cmp)


def to_blocks(a, n_blocks):
    b, t, g, hd = a.shape
    a = jnp.pad(a, ((0, 0), (0, n_blocks * CMP_BLOCK - t), (0, 0), (0, 0)))
    return a.reshape(b, n_blocks, CMP_BLOCK, g, hd).transpose(0, 3, 1, 2, 4)


def nsa_attend(q, gates, q_pos, kc, vc, ks_blk, vs_blk, kw, vw, w_pos, rel_bias):
    b, qb = q.shape[:2]
    G, R = NSA_KV_HEADS, NSA_GROUP
    nc, ns = kc.shape[1], ks_blk.shape[2]
    scale = HEAD_DIM ** -0.5
    qg = q.reshape(b, qb, G, R, HEAD_DIM)
    c_end = (jnp.arange(nc, dtype=jnp.int32) + 1) * CMP_BLOCK - 1
    c_dist = q_pos[:, None] - c_end[None, :]
    lc = jnp.einsum('bqgrd,bngd->bgrqn', qg, kc).astype(jnp.float32) * scale + head_bias(rel_bias, c_dist)
    pc = masked_softmax(lc, c_dist >= 0)
    o_cmp = jnp.einsum('bgrqn,bngd->bqgrd', pc, vc.astype(jnp.float32))
    imp = jnp.pad(jnp.sum(pc, axis=2), ((0, 0), (0, 0), (0, 0), (0, ns - nc)))
    blk = jnp.arange(ns, dtype=jnp.int32)[None, :]
    cur = (q_pos // CMP_BLOCK)[:, None]
    forced = (blk == 0) | (blk == cur) | (blk == cur - 1)
    score = jnp.where(forced, FORCED_SCORE, jnp.where(blk <= cur, imp, -1.0))
    n_sel = min(TOP_N, ns)
    _, idx = lax.top_k(score, n_sel)
    b_ix = jnp.arange(b)[:, None, None, None]
    g_ix = jnp.arange(G)[None, :, None, None]
    ks = ks_blk[b_ix, g_ix, idx]
    vs = vs_blk[b_ix, g_ix, idx]
    s_pos = idx[..., None] * CMP_BLOCK + jnp.arange(CMP_BLOCK, dtype=jnp.int32)
    s_dist = q_pos[None, None, :, None, None] - s_pos
    ls = jnp.einsum('bqgrd,bgqnld->bgrqnl', qg, ks).astype(jnp.float32) * scale
    bucket = t5_bucket(s_dist) * G + g_ix[..., None]
    bias_s = jnp.take(rel_bias.reshape(N_BUCKETS * G, R), bucket, axis=0)
    ls = (ls + jnp.moveaxis(bias_s, -1, 2).astype(jnp.float32)).reshape(b, G, R, qb, n_sel * CMP_BLOCK)
    ps = masked_softmax(ls, (s_dist >= 0).reshape(b, G, 1, qb, n_sel * CMP_BLOCK))
    o_sel = jnp.einsum('bgrqm,bgqmd->bqgrd', ps, vs.reshape(b, G, qb, n_sel * CMP_BLOCK, HEAD_DIM).astype(jnp.float32))
    w_dist = q_pos[:, None] - w_pos[None, :]
    w_mask = (w_dist >= 0) & (w_dist < WINDOW) & (w_pos[None, :] >= 0)
    lw = jnp.einsum('bqgrd,bwgd->bgrqw', qg, kw).astype(jnp.float32) * scale + head_bias(rel_bias, w_dist)
    pw = masked_softmax(lw, w_mask)
    o_win = jnp.einsum('bgrqw,bwgd->bqgrd', pw, vw.astype(jnp.float32))
    g = gates.reshape(b, qb, G, R, N_BRANCH).astype(jnp.float32)
    o = g[..., 0:1] * o_cmp + g[..., 1:2] * o_sel + g[..., 2:3] * o_win
    return o.reshape(b, qb, D_Q_NSA).astype(q.dtype)


def even_mixer(h, conv_hist, past, win_buf, pos0, w_in, conv_w, q_gain, k_gain, w_cmp_k, w_cmp_v, pe_cmp, rel_bias, w_out):
    b, t_new, _ = h.shape
    G, HD = NSA_KV_HEADS, HEAD_DIM
    xa, ba, ca, q, kv, gl = jnp.split(h @ w_in, SPLIT_EVEN, axis=-1)
    conv, new_hist = causal_dwconv(ca * xa, conv_hist, conv_w)
    y_a = ba * conv
    q = rmsnorm(q.reshape(b, t_new, NSA_HEADS, HD), q_gain)
    kv = kv.reshape(b, t_new, 2 * N_BRANCH, G, HD)
    kc_new, vc_new = kv[:, :, 0], kv[:, :, 1]
    ks_new, vs_new = rmsnorm(kv[:, :, 2], k_gain[1]), kv[:, :, 3]
    kw_new, vw_new = rmsnorm(kv[:, :, 4], k_gain[2]), kv[:, :, 5]
    gates = jax.nn.sigmoid(gl.astype(jnp.float32)).reshape(b, t_new, NSA_HEADS, N_BRANCH)
    if past is None:
        kc_all, vc_all, ks_all, vs_all = kc_new, vc_new, ks_new, vs_new
    else:
        kc_all, vc_all, ks_all, vs_all = (jnp.concatenate([p, n], axis=1) for p, n in zip(past, (kc_new, vc_new, ks_new, vs_new)))
    t_all = kc_all.shape[1]
    n_cmp = t_all // CMP_BLOCK
    n_blocks = -(-t_all // CMP_BLOCK)
    kc = rmsnorm(compress_blocks(kc_all, w_cmp_k, pe_cmp, n_cmp), k_gain[0])
    vc = compress_blocks(vc_all, w_cmp_v, pe_cmp, n_cmp)
    ks_blk = to_blocks(ks_all, n_blocks)
    vs_blk = to_blocks(vs_all, n_blocks)
    if win_buf is None:
        band = WINDOW + NSA_QBLOCK
        kw_pad = jnp.pad(kw_new, ((0, 0), (WINDOW, 0), (0, 0), (0, 0)))
        vw_pad = jnp.pad(vw_new, ((0, 0), (WINDOW, 0), (0, 0), (0, 0)))

        def block(i):
            s0 = i * NSA_QBLOCK
            q_pos = s0 + jnp.arange(NSA_QBLOCK, dtype=jnp.int32)
            w_pos = s0 - WINDOW + jnp.arange(band, dtype=jnp.int32)
            return nsa_attend(lax.dynamic_slice_in_dim(q, s0, NSA_QBLOCK, axis=1),
                              lax.dynamic_slice_in_dim(gates, s0, NSA_QBLOCK, axis=1), q_pos, kc, vc, ks_blk, vs_blk,
                              lax.dynamic_slice_in_dim(kw_pad, s0, band, axis=1),
                              lax.dynamic_slice_in_dim(vw_pad, s0, band, axis=1), w_pos, rel_bias)

        o = lax.map(block, jnp.arange(t_new // NSA_QBLOCK, dtype=jnp.int32))
        o = o.transpose(1, 0, 2, 3).reshape(b, t_new, D_Q_NSA)
        n_keep = min(WINDOW, t_new)
        new_win = (kw_new[:, t_new - n_keep:], vw_new[:, t_new - n_keep:])
    else:
        w_buf = win_buf[0].shape[1]
        kw = jnp.concatenate([win_buf[0], kw_new], axis=1)
        vw = jnp.concatenate([win_buf[1], vw_new], axis=1)
        q_pos = pos0 + jnp.arange(t_new, dtype=jnp.int32)
        w_pos = jnp.concatenate([pos0 - w_buf + jnp.arange(w_buf, dtype=jnp.int32), q_pos])
        o = nsa_attend(q, gates, q_pos, kc, vc, ks_blk, vs_blk, kw, vw, w_pos, rel_bias)
        new_win = (kw[:, kw.shape[1] - w_buf:], vw[:, vw.shape[1] - w_buf:])
    out = jnp.concatenate([y_a, o.astype(y_a.dtype)], axis=-1) @ w_out
    return out, new_hist, (kc_new, vc_new, ks_new, vs_new), new_win


def sb_attend(q, q_pos, keys, values, k_pos):
    b, qb = q.shape[:2]
    scale = HEAD_DIM ** -0.5
    z = jnp.concatenate([jnp.einsum('bqhd,bkhd->bhqk', q, k).astype(jnp.float32) for k in keys], axis=-1) * scale
    pos = jnp.concatenate(k_pos)
    valid = pos[None, :] < q_pos[:, None]
    log_keep = jnp.where(valid, jax.nn.log_sigmoid(-z), 0.0)
    suffix = lax.cumsum(log_keep, axis=3, reverse=True) - log_keep
    a = jnp.where(valid, jnp.exp(jax.nn.log_sigmoid(z) + suffix), 0.0)
    splits = [int(c) for c in np.cumsum([k.shape[1] for k in keys])[:-1]]
    o = None
    for a_part, v in zip(jnp.split(a, splits, axis=-1), values):
        term = jnp.einsum('bhqk,bkhd->bqhd', a_part, v.astype(jnp.float32))
        o = term if o is None else o + term
    return o.reshape(b, qb, D_SB).astype(q.dtype)


def odd_mixer(h, past, pos0, w_qkv, w_out):
    b, t_new, _ = h.shape
    qkv = (h @ w_qkv).reshape(b, t_new, 3, SB_HEADS, HEAD_DIM)
    q, k, v = qkv[:, :, 0], qkv[:, :, 1], qkv[:, :, 2]
    new_pos = pos0 + jnp.arange(t_new, dtype=jnp.int32)
    if past is None:
        def block(i):
            s0 = i * SB_QBLOCK
            return sb_attend(lax.dynamic_slice_in_dim(q, s0, SB_QBLOCK, axis=1),
                             s0 + jnp.arange(SB_QBLOCK, dtype=jnp.int32), (k,), (v,), (new_pos,))

        o = lax.map(block, jnp.arange(t_new // SB_QBLOCK, dtype=jnp.int32))
        o = o.transpose(1, 0, 2, 3).reshape(b, t_new, D_SB)
    else:
        past_pos = jnp.arange(past[0].shape[1], dtype=jnp.int32)
        o = sb_attend(q, new_pos, (past[0], k), (past[1], v), (past_pos, new_pos))
    return o @ w_out, k, v


def conv_ffn(h, hist, w_gate, w_up, conv_w, w_down):
    gc, new_hist = causal_dwconv(h @ w_gate, hist, conv_w)
    return (jax.nn.silu(gc) * (h @ w_up)) @ w_down, new_hist


def setup_inputs(seed: int = 0) -> dict:
    key = jax.random.key(seed)
    ks = jax.random.split(key, 32)
    n_pages = PAST_LEN // PAGE_SIZE
    in_use = DEC_BATCH * n_pages
    n_pool = in_use + max(1, in_use // 4)
    w_buf = min(WINDOW, PAST_LEN)

    def nrm(k, shape, scale=1.0):
        return jax.random.normal(k, shape, jnp.float32) * scale

    page_table = jax.random.permutation(ks[0], n_pool)[:in_use].reshape(DEC_BATCH, n_pages).astype(jnp.int32)
    return {
        'x_prompt': nrm(ks[1], (BATCH, SEQ, D_MODEL)),
        'x_sample': nrm(ks[2], (DEC_BATCH, DEC_SEQ, D_MODEL)),
        'state_conv_a': nrm(ks[3], (N_EVEN, DEC_BATCH, CONV_W - 1, D_CONV)),
        'cache_cmp_k': nrm(ks[4], (N_EVEN, n_pool, PAGE_SIZE, NSA_KV_HEADS, HEAD_DIM)),
        'cache_cmp_v': nrm(ks[5], (N_EVEN, n_pool, PAGE_SIZE, NSA_KV_HEADS, HEAD_DIM)),
        'cache_sel_k': nrm(ks[6], (N_EVEN, n_pool, PAGE_SIZE, NSA_KV_HEADS, HEAD_DIM)),
        'cache_sel_v': nrm(ks[7], (N_EVEN, n_pool, PAGE_SIZE, NSA_KV_HEADS, HEAD_DIM)),
        'state_win_k': nrm(ks[8], (N_EVEN, DEC_BATCH, w_buf, NSA_KV_HEADS, HEAD_DIM)),
        'state_win_v': nrm(ks[9], (N_EVEN, DEC_BATCH, w_buf, NSA_KV_HEADS, HEAD_DIM)),
        'cache_sb_k': nrm(ks[10], (N_ODD, n_pool, PAGE_SIZE, SB_HEADS, HEAD_DIM)),
        'cache_sb_v': nrm(ks[11], (N_ODD, n_pool, PAGE_SIZE, SB_HEADS, HEAD_DIM)),
        'state_ffn_conv': nrm(ks[12], (DEPTH, DEC_BATCH, FFN_CONV_W - 1, D_FF)),
        'page_table': page_table,
        'norm_mix': 1.0 + nrm(ks[13], (DEPTH, D_MODEL), 0.01),
        'norm_ffn': 1.0 + nrm(ks[14], (DEPTH, D_MODEL), 0.01),
        'w_in_even': nrm(ks[15], (N_EVEN, D_MODEL, D_IN_EVEN), D_MODEL ** -0.5),
        'conv_a_w': nrm(ks[16], (N_EVEN, CONV_W, D_CONV), CONV_W ** -0.5),
        'q_norm': 1.0 + nrm(ks[17], (N_EVEN, HEAD_DIM), 0.01),
        'k_norm': 1.0 + nrm(ks[18], (N_EVEN, N_BRANCH, HEAD_DIM), 0.01),
        'w_cmp_k': nrm(ks[19], (N_EVEN, CMP_BLOCK, HEAD_DIM, HEAD_DIM), (CMP_BLOCK * HEAD_DIM) ** -0.5),
        'w_cmp_v': nrm(ks[20], (N_EVEN, CMP_BLOCK, HEAD_DIM, HEAD_DIM), (CMP_BLOCK * HEAD_DIM) ** -0.5),
        'pe_cmp': nrm(ks[21], (N_EVEN, CMP_BLOCK, HEAD_DIM), 0.1),
        'rel_bias': nrm(ks[22], (N_BUCKETS, NSA_HEADS), 0.2),
        'w_out_even': nrm(ks[23], (N_EVEN, D_MIX_EVEN, D_MODEL), D_MIX_EVEN ** -0.5),
        'w_qkv_odd': nrm(ks[24], (N_ODD, D_MODEL, 3 * D_SB), D_MODEL ** -0.5),
        'w_out_odd': nrm(ks[25], (N_ODD, D_SB, D_MODEL), D_SB ** -0.5),
        'w_ffn_gate': nrm(ks[26], (DEPTH, D_MODEL, D_FF), D_MODEL ** -0.5),
        'w_ffn_up': nrm(ks[27], (DEPTH, D_MODEL, D_FF), D_MODEL ** -0.5),
        'ffn_conv_w': nrm(ks[28], (DEPTH, FFN_CONV_W, D_FF), FFN_CONV_W ** -0.5),
        'w_ffn_down': nrm(ks[29], (DEPTH, D_FF, D_MODEL), D_FF ** -0.5),
    }


def reference(x_prompt, x_sample, state_conv_a, cache_cmp_k, cache_cmp_v, cache_sel_k, cache_sel_v,
              state_win_k, state_win_v, cache_sb_k, cache_sb_v, state_ffn_conv, page_table,
              norm_mix, norm_ffn, w_in_even, conv_a_w, q_norm, k_norm, w_cmp_k, w_cmp_v, pe_cmp,
              rel_bias, w_out_even, w_qkv_odd, w_out_odd, w_ffn_gate, w_ffn_up, ffn_conv_w, w_ffn_down):
    past_len = page_table.shape[1] * PAGE_SIZE
    xp, xs = x_prompt, x_sample
    bp = xp.shape[0]
    conv_p, conv_s = [], []
    cmp_kp, cmp_vp, cmp_ks, cmp_vs = [], [], [], []
    sel_kp, sel_vp, sel_ks, sel_vs = [], [], [], []
    win_kp, win_vp, win_ks, win_vs = [], [], [], []
    sb_kp, sb_vp, sb_ks, sb_vs = [], [], [], []
    ffn_p, ffn_s = [], []
    for layer in range(DEPTH):
        hp = rmsnorm(xp, norm_mix[layer])
        hs = rmsnorm(xs, norm_mix[layer])
        if layer % 2 == 0:
            e = layer // 2
            wts = (w_in_even[e], conv_a_w[e], q_norm[e], k_norm[e], w_cmp_k[e], w_cmp_v[e], pe_cmp[e], rel_bias, w_out_even[e])
            hist0 = jnp.zeros((bp, CONV_W - 1, D_CONV), xp.dtype)
            mp, hist_p, rows_p, win_p = even_mixer(hp, hist0, None, None, 0, *wts)
            past = tuple(gather_pages(c[e], page_table) for c in (cache_cmp_k, cache_cmp_v, cache_sel_k, cache_sel_v))
            ms, hist_s, rows_s, win_s = even_mixer(hs, state_conv_a[e], past, (state_win_k[e], state_win_v[e]), past_len, *wts)
            conv_p.append(hist_p)
            conv_s.append(hist_s)
            cmp_kp.append(rows_p[0]); cmp_vp.append(rows_p[1]); sel_kp.append(rows_p[2]); sel_vp.append(rows_p[3])
            cmp_ks.append(rows_s[0]); cmp_vs.append(rows_s[1]); sel_ks.append(rows_s[2]); sel_vs.append(rows_s[3])
            win_kp.append(win_p[0]); win_vp.append(win_p[1]); win_ks.append(win_s[0]); win_vs.append(win_s[1])
        else:
            o = layer // 2
            mp, kp_new, vp_new = odd_mixer(hp, None, 0, w_qkv_odd[o], w_out_odd[o])
            past = (gather_pages(cache_sb_k[o], page_table), gather_pages(cache_sb_v[o], page_table))
            ms, ks_new, vs_new = odd_mixer(hs, past, past_len, w_qkv_odd[o], w_out_odd[o])
            sb_kp.append(kp_new); sb_vp.append(vp_new); sb_ks.append(ks_new); sb_vs.append(vs_new)
        xp = xp + mp
        xs = xs + ms
        fp, hf_p = conv_ffn(rmsnorm(xp, norm_ffn[layer]), jnp.zeros((bp, FFN_CONV_W - 1, D_FF), xp.dtype),
                            w_ffn_gate[layer], w_ffn_up[layer], ffn_conv_w[layer], w_ffn_down[layer])
        fs, hf_s = conv_ffn(rmsnorm(xs, norm_ffn[layer]), state_ffn_conv[layer],
                            w_ffn_gate[layer], w_ffn_up[layer], ffn_conv_w[layer], w_ffn_down[layer])
        ffn_p.append(hf_p)
        ffn_s.append(hf_s)
        xp = xp + fp
        xs = xs + fs
    st = lambda lst: jnp.stack(lst, axis=0)
    return (xp, xs, st(conv_p), st(conv_s),
            st(cmp_kp), st(cmp_vp), st(cmp_ks), st(cmp_vs),
            st(sel_kp), st(sel_vp), st(sel_ks), st(sel_vs),
            st(win_kp), st(win_vp), st(win_ks), st(win_vs),
            st(sb_kp), st(sb_vp), st(sb_ks), st(sb_vs),
            st(ffn_p), st(ffn_s))
```

```python
import functools
import math

import jax
import jax.numpy as jnp
from jax import lax
from jax.experimental import pallas as pl
from jax.experimental.pallas import tpu as pltpu

F32 = jnp.float32
BF16 = jnp.bfloat16

HEAD_DIM = 128
LANES = 128
PAGE_SIZE = 128
CMP_BLOCK = 64
TOP_N = 16
WINDOW = 512
NSA_GROUP = 4
N_BRANCH = 3
N_BUCKETS = 32
MAX_DISTANCE = 128
MAX_EXACT = N_BUCKETS // 2
FAR_BUCKET = N_BUCKETS - 1
FORCED_SCORE = float(NSA_GROUP + 1)
RMS_EPS = 1e-6
NEG = -1e30
ATT_SCALE = HEAD_DIM ** -0.5
TILE = 128
VMEM_LIMIT = 56 * 1024 * 1024


def _params(*sem):
    return pltpu.CompilerParams(dimension_semantics=sem, vmem_limit_bytes=VMEM_LIMIT)


def _dot(a, b):
    return jnp.dot(a, b, preferred_element_type=F32)


def _dot_nt(a, b):
    return lax.dot_general(a, b, (((1,), (1,)), ((), ())), preferred_element_type=F32)


def _rms(x, gain):
    return x * lax.rsqrt(jnp.mean(x * x, axis=-1, keepdims=True) + RMS_EPS) * gain


def _sigmoid(x):
    return 1.0 / (1.0 + jnp.exp(-x))


def _t5_bucket(dist):
    n = jnp.maximum(dist, 0)
    nf = jnp.maximum(n, 1).astype(F32)
    large = MAX_EXACT + (jnp.log(nf / MAX_EXACT) / math.log(MAX_DISTANCE / MAX_EXACT)
                         * (N_BUCKETS - MAX_EXACT)).astype(jnp.int32)
    large = jnp.minimum(large, N_BUCKETS - 1)
    return jnp.where(n < MAX_EXACT, n, large)


def _bias_from_bucket(bucket, rb_ref, head):
    out = jnp.zeros(bucket.shape, F32)
    for k in range(N_BUCKETS):
        out = jnp.where(bucket == k, rb_ref[k, head], out)
    return out


def _masked_softmax(l, mask):
    l = jnp.where(mask, l, NEG)
    m = jnp.max(l, axis=-1, keepdims=True)
    e = jnp.where(mask, jnp.exp(l - m), 0.0)
    return e / jnp.maximum(jnp.sum(e, axis=-1, keepdims=True), 1e-30)


def _log_sigmoid(z):
    return jnp.minimum(z, 0.0) - jnp.log1p(jnp.exp(-jnp.abs(z)))


def _column(x, c):
    lane = lax.broadcasted_iota(jnp.int32, x.shape, 1)
    return jnp.sum(jnp.where(lane == c, x, 0.0), axis=1, keepdims=True)


def _round_up(n, m):
    return -(-n // m) * m


def _rmsnorm_kernel(x_ref, g_ref, o_ref):
    o_ref[...] = _rms(x_ref[...], g_ref[...]).astype(o_ref.dtype)


def rmsnorm_bf16(x, gain):
    m, d = x.shape
    tr = min(m, 256)
    return pl.pallas_call(
        _rmsnorm_kernel,
        grid=(m // tr,),
        in_specs=[pl.BlockSpec((tr, d), lambda i: (i, 0)),
                  pl.BlockSpec((1, d), lambda i: (0, 0))],
        out_specs=pl.BlockSpec((tr, d), lambda i: (i, 0)),
        out_shape=jax.ShapeDtypeStruct((m, d), BF16),
        compiler_params=_params("parallel"),
        name="rmsnorm",
    )(x, gain.reshape(1, d))


def _matmul_kernel(*refs, n_x, k_sizes, has_res, nk):
    x_refs = refs[:n_x]
    w_ref = refs[n_x]
    r_ref = refs[n_x + 1] if has_res else None
    o_ref = refs[n_x + 1 + int(has_res)]
    acc_ref = refs[-1] if nk > 1 else None

    acc = None
    k0 = 0
    for x_ref, ks in zip(x_refs, k_sizes):
        part = _dot(x_ref[...], w_ref[k0:k0 + ks, :])
        acc = part if acc is None else acc + part
        k0 += ks

    def finish(total):
        if has_res:
            total = total + r_ref[...]
        o_ref[...] = total.astype(o_ref.dtype)

    if nk == 1:
        finish(acc)
    else:
        k = pl.program_id(2)

        @pl.when(k == 0)
        def _():
            acc_ref[...] = acc

        @pl.when(k > 0)
        def _():
            acc_ref[...] += acc

        @pl.when(k == nk - 1)
        def _():
            finish(acc_ref[...])


def matmul(xs, w, res=None, *, tm, tn, nk=1, out_dtype=F32, name="matmul"):
    m = xs[0].shape[0]
    k_sizes = tuple(x.shape[1] for x in xs)
    k_total, n = w.shape
    assert sum(k_sizes) == k_total and m % tm == 0 and n % tn == 0
    assert nk == 1 or len(xs) == 1
    tk = k_total // nk
    if nk == 1:
        x_specs = [pl.BlockSpec((tm, ks), lambda i, j, k: (i, 0)) for ks in k_sizes]
        k_blk = k_sizes
    else:
        x_specs = [pl.BlockSpec((tm, tk), lambda i, j, k: (i, k))]
        k_blk = (tk,)
    in_specs = x_specs + [pl.BlockSpec((tk, tn), lambda i, j, k: (k, j))]
    args = list(xs) + [w]
    if res is not None:
        in_specs.append(pl.BlockSpec((tm, tn), lambda i, j, k: (i, j)))
        args.append(res)
    scratch = [pltpu.VMEM((tm, tn), F32)] if nk > 1 else []
    return pl.pallas_call(
        functools.partial(_matmul_kernel, n_x=len(xs), k_sizes=k_blk, has_res=res is not None, nk=nk),
        grid=(m // tm, n // tn, nk),
        in_specs=in_specs,
        out_specs=pl.BlockSpec((tm, tn), lambda i, j, k: (i, j)),
        out_shape=jax.ShapeDtypeStruct((m, n), out_dtype),
        scratch_shapes=scratch,
        compiler_params=_params("parallel", "parallel", "arbitrary"),
        name=name,
    )(*args)


def _conv3(u, hist, w):
    t = u.shape[0]
    row = lax.broadcasted_iota(jnp.int32, u.shape, 0)
    h0, h1 = hist[0:1], hist[1:2]
    u1 = jnp.where(row == 0, h1, pltpu.roll(u, 1, 0))
    u2 = jnp.where(row == 0, h0, jnp.where(row == 1, h1, pltpu.roll(u, 2, 0)))
    y = w[2:3] * u
    y = y + w[0:1] * u2
    y = y + w[1:2] * u1
    return y, u[t - 2:t]


def _conv_gate_kernel(xa_ref, ba_ref, ca_ref, hist_ref, w_ref, y_ref, nh_ref):
    u = ca_ref[...] * xa_ref[...]
    conv, new_hist = _conv3(u, hist_ref[0], w_ref[...])
    y_ref[...] = (ba_ref[...] * conv).astype(y_ref.dtype)
    nh_ref[0] = new_hist


def conv_gate(proj, hist, w, *, n_batch, t, d_conv, out_dtype, tc=256):
    nb = d_conv // tc
    return pl.pallas_call(
        _conv_gate_kernel,
        grid=(n_batch, nb),
        in_specs=[pl.BlockSpec((t, tc), lambda b, j: (b, j)),
                  pl.BlockSpec((t, tc), lambda b, j: (b, nb + j)),
                  pl.BlockSpec((t, tc), lambda b, j: (b, 2 * nb + j)),
                  pl.BlockSpec((1, 2, tc), lambda b, j: (b, 0, j)),
                  pl.BlockSpec((3, tc), lambda b, j: (0, j))],
        out_specs=[pl.BlockSpec((t, tc), lambda b, j: (b, j)),
                   pl.BlockSpec((1, 2, tc), lambda b, j: (b, 0, j))],
        out_shape=[jax.ShapeDtypeStruct((n_batch * t, d_conv), out_dtype),
                   jax.ShapeDtypeStruct((n_batch, 2, d_conv), F32)],
        compiler_params=_params("parallel", "parallel"),
        name="conv_gate",
    )(proj, proj, proj, hist, w)


def _ffn_act_kernel(g_ref, u_ref, hist_ref, w_ref, a_ref, nh_ref):
    gate = g_ref[...]
    gc, new_hist = _conv3(gate, hist_ref[0], w_ref[...])
    a_ref[...] = (gc * _sigmoid(gc) * u_ref[...]).astype(a_ref.dtype)
    nh_ref[0] = new_hist


def ffn_act(gu, hist, w, *, n_batch, t, d_ff, out_dtype, tc=256):
    nb = d_ff // tc
    return pl.pallas_call(
        _ffn_act_kernel,
        grid=(n_batch, nb),
        in_specs=[pl.BlockSpec((t, tc), lambda b, j: (b, j)),
                  pl.BlockSpec((t, tc), lambda b, j: (b, nb + j)),
                  pl.BlockSpec((1, 2, tc), lambda b, j: (b, 0, j)),
                  pl.BlockSpec((3, tc), lambda b, j: (0, j))],
        out_specs=[pl.BlockSpec((t, tc), lambda b, j: (b, j)),
                   pl.BlockSpec((1, 2, tc), lambda b, j: (b, 0, j))],
        out_shape=[jax.ShapeDtypeStruct((n_batch * t, d_ff), out_dtype),
                   jax.ShapeDtypeStruct((n_batch, 2, d_ff), F32)],
        compiler_params=_params("parallel", "parallel"),
        name="ffn_act",
    )(gu, gu, hist, w)


def _headnorm_kernel(x_ref, g_ref, o_ref):
    o_ref[...] = _rms(x_ref[...], g_ref[...])


def headnorm(src, col0, n_heads, gain):
    m = src.shape[0]
    tr = min(m, 512)
    cb = col0 // HEAD_DIM
    return pl.pallas_call(
        _headnorm_kernel,
        grid=(m // tr, n_heads),
        in_specs=[pl.BlockSpec((tr, HEAD_DIM), lambda i, h: (i, cb + h)),
                  pl.BlockSpec((1, HEAD_DIM), lambda i, h: (0, 0))],
        out_specs=pl.BlockSpec((tr, HEAD_DIM), lambda i, h: (i, h)),
        out_shape=jax.ShapeDtypeStruct((m, n_heads * HEAD_DIM), F32),
        compiler_params=_params("parallel", "parallel"),
        name="headnorm",
    )(src, gain.reshape(1, HEAD_DIM))


def _compress_kernel(tbl_ref, k_ref, v_ref, pe_ref, wk_ref, wv_ref, g_ref, kc_ref, vc_ref, ks_ref, vs_ref, *,
                     chunk, n_groups):
    del tbl_ref
    slot = pl.program_id(1) % chunk
    off = pl.multiple_of(slot * PAGE_SIZE, PAGE_SIZE)
    for g in range(n_groups):
        ks_ref[g, pl.ds(off, PAGE_SIZE), :] = k_ref[:, g * HEAD_DIM:(g + 1) * HEAD_DIM] + pe_ref[...]
        vs_ref[g, pl.ds(off, PAGE_SIZE), :] = v_ref[:, g * HEAD_DIM:(g + 1) * HEAD_DIM] + pe_ref[...]
    nb = chunk * (PAGE_SIZE // CMP_BLOCK)

    @pl.when(slot == chunk - 1)
    def _():
        for s_ref, w_ref, o_ref, norm in ((ks_ref, wk_ref, kc_ref, True), (vs_ref, wv_ref, vc_ref, False)):
            def body(l, acc, s_ref=s_ref, w_ref=w_ref):
                a = jnp.concatenate([s_ref[g, pl.ds(l, nb, stride=CMP_BLOCK), :] for g in range(n_groups)], axis=0)
                return acc + _dot(a.astype(BF16), w_ref[l])

            acc = lax.fori_loop(0, CMP_BLOCK, body, jnp.zeros((n_groups * nb, HEAD_DIM), F32))
            if norm:
                acc = _rms(acc, g_ref[...])
            for g in range(n_groups):
                o_ref[0, g] = acc[g * nb:(g + 1) * nb]


def compress(src_k, src_v, col_k, col_v, table, pe, wk, wv, gain, *, n_batch, n_pages, chunk, n_groups):
    width = n_groups * HEAD_DIM
    nb_total = n_pages * (PAGE_SIZE // CMP_BLOCK)
    nb_chunk = chunk * (PAGE_SIZE // CMP_BLOCK)
    ck, cv = col_k // width, col_v // width
    pe_t = jnp.tile(pe, (PAGE_SIZE // CMP_BLOCK, 1))
    grid_spec = pltpu.PrefetchScalarGridSpec(
        num_scalar_prefetch=1,
        grid=(n_batch, n_pages),
        in_specs=[pl.BlockSpec((PAGE_SIZE, width), lambda b, p, t: (t[b * n_pages + p], ck)),
                  pl.BlockSpec((PAGE_SIZE, width), lambda b, p, t: (t[b * n_pages + p], cv)),
                  pl.BlockSpec((PAGE_SIZE, HEAD_DIM), lambda b, p, t: (0, 0)),
                  pl.BlockSpec((CMP_BLOCK, HEAD_DIM, HEAD_DIM), lambda b, p, t: (0, 0, 0)),
                  pl.BlockSpec((CMP_BLOCK, HEAD_DIM, HEAD_DIM), lambda b, p, t: (0, 0, 0)),
                  pl.BlockSpec((1, HEAD_DIM), lambda b, p, t: (0, 0))],
        out_specs=[pl.BlockSpec((1, n_groups, nb_chunk, HEAD_DIM), lambda b, p, t: (b, 0, p // chunk, 0)),
                   pl.BlockSpec((1, n_groups, nb_chunk, HEAD_DIM), lambda b, p, t: (b, 0, p // chunk, 0))],
        scratch_shapes=[pltpu.VMEM((n_groups, chunk * PAGE_SIZE, HEAD_DIM), F32),
                        pltpu.VMEM((n_groups, chunk * PAGE_SIZE, HEAD_DIM), F32)],
    )
    out = jax.ShapeDtypeStruct((n_batch, n_groups, nb_total, HEAD_DIM), F32)
    return pl.pallas_call(
        functools.partial(_compress_kernel, chunk=chunk, n_groups=n_groups),
        grid_spec=grid_spec,
        out_shape=[out, out],
        compiler_params=_params("parallel", "arbitrary"),
        name="compress",
    )(table, src_k, src_v, pe_t, wk, wv, gain.reshape(1, HEAD_DIM))


def _bias_tiles_kernel(rb_ref, o_ref):
    h = pl.program_id(0)
    i = lax.broadcasted_iota(jnp.int32, (TILE, TILE), 0)
    j = lax.broadcasted_iota(jnp.int32, (TILE, TILE), 1)
    for m in range(2):
        o_ref[0, m] = _bias_from_bucket(_t5_bucket(i - j + TILE * m), rb_ref, h)


def bias_tiles(rel_bias):
    n_heads = rel_bias.shape[1]
    return pl.pallas_call(
        _bias_tiles_kernel,
        grid=(n_heads,),
        in_specs=[pl.BlockSpec(memory_space=pltpu.SMEM)],
        out_specs=pl.BlockSpec((1, 2, TILE, TILE), lambda h: (h, 0, 0, 0)),
        out_shape=jax.ShapeDtypeStruct((n_heads, 2, TILE, TILE), F32),
        compiler_params=_params("arbitrary"),
        name="bias_tiles",
    )(rel_bias)


def _select_blocks(imp, q_pos, ns):
    blk = lax.broadcasted_iota(jnp.int32, imp.shape, 1)
    cur = q_pos // CMP_BLOCK
    forced = (blk == 0) | (blk == cur) | (blk == cur - 1)
    score = jnp.where(forced, FORCED_SCORE, jnp.where(blk <= cur, imp, -1.0))
    cnt = jnp.zeros(imp.shape, jnp.int32)
    for i in range(ns):
        ci = score[:, i:i + 1]
        ahead = (ci > score) | ((ci == score) & (blk > i))
        cnt = cnt + jnp.where(ahead, 1, 0)
    return jnp.where((cnt < min(TOP_N, ns)) & (blk < ns), 1.0, 0.0)


def _nsa_prompt_kernel(rb_ref, q_ref, qg_ref, gl_ref, kc_ref, vc_ref, ks_ref, vs_ref, kw_ref, vw_ref, d_ref,
                       o_ref, selt_ref, *, n_cmp, n_kt):
    g = pl.program_id(1)
    qt = pl.program_id(2)
    R = NSA_GROUP
    q = q_ref[...]
    qn = jnp.concatenate([_rms(q[:, r * HEAD_DIM:(r + 1) * HEAD_DIM], qg_ref[...]) for r in range(R)], axis=0)
    qb = qn.astype(BF16)
    gates = _sigmoid(gl_ref[...])
    ii = lax.broadcasted_iota(jnp.int32, (TILE, TILE), 0)
    jj = lax.broadcasted_iota(jnp.int32, (TILE, TILE), 1)
    far = [rb_ref[FAR_BUCKET, g * R + r] for r in range(R)]

    q_pos = qt * TILE + lax.broadcasted_iota(jnp.int32, (TILE, 1), 0)
    c_end = (lax.broadcasted_iota(jnp.int32, (1, n_cmp), 1) + 1) * CMP_BLOCK - 1
    c_dist = q_pos - c_end
    c_bucket = _t5_bucket(c_dist)
    lc = _dot_nt(qb, kc_ref[...].astype(BF16)) * ATT_SCALE
    vcb = vc_ref[...].astype(BF16)
    imp = jnp.zeros((TILE, n_cmp), F32)
    o_cmp = []
    for r in range(R):
        l_r = lc[r * TILE:(r + 1) * TILE] + _bias_from_bucket(c_bucket, rb_ref, g * R + r)
        p_r = _masked_softmax(l_r, c_dist >= 0)
        imp = imp + p_r
        o_cmp.append(_dot(p_r.astype(BF16), vcb))

    selb = _select_blocks(imp, q_pos, n_cmp).astype(BF16)
    bn = lax.broadcasted_iota(jnp.int32, (n_cmp, TILE), 0)
    bj = lax.broadcasted_iota(jnp.int32, (n_cmp, TILE), 1)
    for kt in range(n_kt):
        expand = jnp.where(bn == kt * (TILE // CMP_BLOCK) + bj // CMP_BLOCK, 1.0, 0.0).astype(BF16)
        selt_ref[kt] = _dot(selb, expand)

    def tile_bias(r, rel):
        if isinstance(rel, int):
            return d_ref[r, rel] if rel < 2 else far[r]
        return jnp.where(rel == 0, d_ref[r, 0], jnp.where(rel == 1, d_ref[r, 1], far[r]))

    def sel_body(kt, carry):
        ms, ls, accs = carry
        rel = qt - kt
        start = pl.multiple_of(kt * TILE, TILE)
        k = ks_ref[pl.ds(start, TILE), :].astype(BF16)
        v = vs_ref[pl.ds(start, TILE), :].astype(BF16)
        s = _dot_nt(qb, k) * ATT_SCALE
        mask = (selt_ref[kt] > 0.5) & (ii - jj + TILE * rel >= 0)
        new_m, new_l, new_acc = [], [], []
        for r in range(R):
            s_r = jnp.where(mask, s[r * TILE:(r + 1) * TILE] + tile_bias(r, rel), NEG)
            m_new = jnp.maximum(ms[r], jnp.max(s_r, axis=-1, keepdims=True))
            alpha = jnp.exp(ms[r] - m_new)
            p = jnp.where(mask, jnp.exp(s_r - m_new), 0.0)
            new_m.append(m_new)
            new_l.append(alpha * ls[r] + jnp.sum(p, axis=-1, keepdims=True))
            new_acc.append(alpha * accs[r] + _dot(p.astype(BF16), v))
        return tuple(new_m), tuple(new_l), tuple(new_acc)

    init = (tuple(jnp.full((TILE, 1), NEG, F32) for _ in range(R)),
            tuple(jnp.zeros((TILE, 1), F32) for _ in range(R)),
            tuple(jnp.zeros((TILE, HEAD_DIM), F32) for _ in range(R)))
    _, l_sel, acc_sel = lax.fori_loop(0, qt + 1, sel_body, init)

    n_rel = WINDOW // TILE + 1
    s_tiles, v_tiles, masks = [], [], []
    for rel in range(n_rel):
        kt = qt - rel
        start = pl.multiple_of(jnp.maximum(kt, 0) * TILE, TILE)
        k = kw_ref[pl.ds(start, TILE), :].astype(BF16)
        v_tiles.append(vw_ref[pl.ds(start, TILE), :].astype(BF16))
        s_tiles.append(_dot_nt(qb, k) * ATT_SCALE)
        dist = ii - jj + TILE * rel + jnp.where(kt >= 0, 0, WINDOW)
        masks.append((dist >= 0) & (dist < WINDOW))
    vwin = jnp.concatenate(v_tiles, axis=0)
    wmask = jnp.concatenate([jnp.where(m, 1.0, 0.0) for m in masks], axis=1) > 0.5

    for r in range(R):
        lw = jnp.concatenate([s_tiles[rel][r * TILE:(r + 1) * TILE] + tile_bias(r, rel)
                              for rel in range(n_rel)], axis=1)
        pw = _masked_softmax(lw, wmask)
        o_win = _dot(pw.astype(BF16), vwin)
        o_sel = acc_sel[r] / jnp.maximum(l_sel[r], 1e-30)
        c = (g * R + r) * N_BRANCH
        o = _column(gates, c) * o_cmp[r] + _column(gates, c + 1) * o_sel + _column(gates, c + 2) * o_win
        o_ref[:, r * HEAD_DIM:(r + 1) * HEAD_DIM] = o.astype(o_ref.dtype)


def nsa_prompt(proj, gl, kc, vc, ks, kw, dt, rel_bias, q_gain, *, n_batch, t, n_groups, col_q, col_vs, col_vw):
    n_qt = t // TILE
    n_cmp = kc.shape[2]
    gw = NSA_GROUP * HEAD_DIM
    cq, cvs, cvw = col_q // gw, col_vs // HEAD_DIM, col_vw // HEAD_DIM
    return pl.pallas_call(
        functools.partial(_nsa_prompt_kernel, n_cmp=n_cmp, n_kt=n_qt),
        grid=(n_batch, n_groups, n_qt),
        in_specs=[pl.BlockSpec(memory_space=pltpu.SMEM),
                  pl.BlockSpec((TILE, gw), lambda b, g, i: (b * n_qt + i, cq + g)),
                  pl.BlockSpec((1, HEAD_DIM), lambda b, g, i: (0, 0)),
                  pl.BlockSpec((TILE, LANES), lambda b, g, i: (b * n_qt + i, 0)),
                  pl.BlockSpec((None, None, n_cmp, HEAD_DIM), lambda b, g, i: (b, g, 0, 0)),
                  pl.BlockSpec((None, None, n_cmp, HEAD_DIM), lambda b, g, i: (b, g, 0, 0)),
                  pl.BlockSpec((t, HEAD_DIM), lambda b, g, i: (b, g)),
                  pl.BlockSpec((t, HEAD_DIM), lambda b, g, i: (b, cvs + g)),
                  pl.BlockSpec((t, HEAD_DIM), lambda b, g, i: (b, g)),
                  pl.BlockSpec((t, HEAD_DIM), lambda b, g, i: (b, cvw + g)),
                  pl.BlockSpec((NSA_GROUP, 2, TILE, TILE), lambda b, g, i: (g, 0, 0, 0))],
        out_specs=pl.BlockSpec((TILE, gw), lambda b, g, i: (b * n_qt + i, g)),
        out_shape=jax.ShapeDtypeStruct((n_batch * t, n_groups * gw), BF16),
        scratch_shapes=[pltpu.VMEM((n_qt, TILE, TILE), F32)],
        compiler_params=_params("parallel", "parallel", "arbitrary"),
        name="nsa_prompt",
    )(rel_bias, proj, q_gain.reshape(1, HEAD_DIM), gl, kc, vc, ks, proj, kw, proj, dt)


def _nsa_sample_a_kernel(rb_ref, q_ref, qg_ref, gl_ref, kc_ref, vc_ref, wk_ref, wv_ref, kn_ref, vn_ref,
                         qn_ref, part_ref, sel_ref, *, ns, t_new, past_len, w_buf):
    g = pl.program_id(1)
    R = NSA_GROUP
    n_cols = kc_ref.shape[0]
    q = q_ref[...]
    qn = jnp.concatenate([_rms(q[:, r * HEAD_DIM:(r + 1) * HEAD_DIM], qg_ref[...]) for r in range(R)], axis=0)
    qb = qn.astype(BF16)
    gates = _sigmoid(gl_ref[...])
    q_pos = past_len + lax.broadcasted_iota(jnp.int32, (t_new, 1), 0)

    c_end = (lax.broadcasted_iota(jnp.int32, (1, n_cols), 1) + 1) * CMP_BLOCK - 1
    c_dist = q_pos - c_end
    c_bucket = _t5_bucket(c_dist)
    lc = _dot_nt(qb, kc_ref[...].astype(BF16)) * ATT_SCALE
    vcb = vc_ref[...].astype(BF16)
    imp = jnp.zeros((t_new, n_cols), F32)
    o_cmp = []
    for r in range(R):
        l_r = lc[r * t_new:(r + 1) * t_new] + _bias_from_bucket(c_bucket, rb_ref, g * R + r)
        p_r = _masked_softmax(l_r, c_dist >= 0)
        imp = imp + p_r
        o_cmp.append(_dot(p_r.astype(BF16), vcb))

    sel_ref[...] = _select_blocks(imp, q_pos, ns)

    pad = jnp.zeros((TILE - t_new, HEAD_DIM), F32)
    kwin = jnp.concatenate([wk_ref[...], kn_ref[...], pad], axis=0).astype(BF16)
    vwin = jnp.concatenate([wv_ref[...], vn_ref[...], pad], axis=0).astype(BF16)
    nw = w_buf + TILE
    col = lax.broadcasted_iota(jnp.int32, (1, nw), 1)
    w_pos = past_len - w_buf + col
    w_dist = q_pos - w_pos
    w_mask = (w_dist >= 0) & (w_dist < WINDOW) & (w_pos >= 0) & (col < w_buf + t_new)
    w_bucket = _t5_bucket(w_dist)
    lw = _dot_nt(qb, kwin) * ATT_SCALE
    for r in range(R):
        l_r = lw[r * t_new:(r + 1) * t_new] + _bias_from_bucket(w_bucket, rb_ref, g * R + r)
        o_win = _dot(_masked_softmax(l_r, w_mask).astype(BF16), vwin)
        c = (g * R + r) * N_BRANCH
        part_ref[:, r * HEAD_DIM:(r + 1) * HEAD_DIM] = _column(gates, c) * o_cmp[r] + _column(gates, c + 2) * o_win
        qn_ref[:, r * HEAD_DIM:(r + 1) * HEAD_DIM] = qn[r * t_new:(r + 1) * t_new]


def nsa_sample_a(proj, gl, kc, vc, win_k, win_v, kw_new, rel_bias, q_gain, *, n_batch, t_new, n_groups, past_len,
                 col_q, col_vw):
    ns_pad = kc.shape[2]
    ns = -(-(past_len + t_new) // CMP_BLOCK)
    w_buf = win_k.shape[1]
    gw = NSA_GROUP * HEAD_DIM
    cq, cvw = col_q // gw, col_vw // HEAD_DIM
    d_q = n_groups * gw
    return pl.pallas_call(
        functools.partial(_nsa_sample_a_kernel, ns=ns, t_new=t_new, past_len=past_len, w_buf=w_buf),
        grid=(n_batch, n_groups),
        in_specs=[pl.BlockSpec(memory_space=pltpu.SMEM),
                  pl.BlockSpec((t_new, gw), lambda b, g: (b, cq + g)),
                  pl.BlockSpec((1, HEAD_DIM), lambda b, g: (0, 0)),
                  pl.BlockSpec((t_new, LANES), lambda b, g: (b, 0)),
                  pl.BlockSpec((None, None, ns_pad, HEAD_DIM), lambda b, g: (b, g, 0, 0)),
                  pl.BlockSpec((None, None, ns_pad, HEAD_DIM), lambda b, g: (b, g, 0, 0)),
                  pl.BlockSpec((None, w_buf, HEAD_DIM), lambda b, g: (b, 0, g)),
                  pl.BlockSpec((None, w_buf, HEAD_DIM), lambda b, g: (b, 0, g)),
                  pl.BlockSpec((t_new, HEAD_DIM), lambda b, g: (b, g)),
                  pl.BlockSpec((t_new, HEAD_DIM), lambda b, g: (b, cvw + g))],
        out_specs=[pl.BlockSpec((t_new, gw), lambda b, g: (b, g)),
                   pl.BlockSpec((t_new, gw), lambda b, g: (b, g)),
                   pl.BlockSpec((None, None, t_new, ns_pad), lambda b, g: (b, g, 0, 0))],
        out_shape=[jax.ShapeDtypeStruct((n_batch * t_new, d_q), F32),
                   jax.ShapeDtypeStruct((n_batch * t_new, d_q), F32),
                   jax.ShapeDtypeStruct((n_batch, n_groups, t_new, ns_pad), F32)],
        compiler_params=_params("parallel", "parallel"),
        name="nsa_sample_a",
    )(rel_bias, proj, q_gain.reshape(1, HEAD_DIM), gl, kc, vc, win_k, win_v, kw_new, proj)


def _nsa_sample_b_kernel(tbl_ref, rb_ref, qn_ref, gl_ref, part_ref, mask_ref, nmask_ref, k_ref, v_ref, kn_ref, vn_ref,
                         o_ref, m_ref, l_ref, acc_ref, bias_ref, *, n_pages, n_groups, t_new, past_len):
    del tbl_ref
    p = pl.program_id(1)
    R = NSA_GROUP
    rows_g = R * t_new
    rows = n_groups * rows_g
    row_head = lax.broadcasted_iota(jnp.int32, (rows, 1), 0) // t_new
    n_heads = n_groups * R

    def q_rows(g):
        qn = qn_ref[...]
        return jnp.concatenate([qn[:, (g * R + r) * HEAD_DIM:(g * R + r + 1) * HEAD_DIM] for r in range(R)],
                               axis=0).astype(BF16)

    def near_bias(base):
        qi = lax.broadcasted_iota(jnp.int32, (t_new, TILE), 0)
        kj = lax.broadcasted_iota(jnp.int32, (t_new, TILE), 1)
        bucket = _t5_bucket(qi - kj + base)
        return jnp.concatenate([_bias_from_bucket(bucket, rb_ref, h) for h in range(n_heads)], axis=0)

    @pl.when(p == 0)
    def _():
        m_ref[...] = jnp.full(m_ref.shape, NEG, F32)
        l_ref[...] = jnp.zeros(l_ref.shape, F32)
        acc_ref[...] = jnp.zeros(acc_ref.shape, F32)
        far = jnp.zeros((rows, 1), F32)
        for h in range(n_heads):
            far = jnp.where(row_head == h, rb_ref[FAR_BUCKET, h], far)
        bias_ref[...] = jnp.broadcast_to(far, bias_ref.shape)

    @pl.when(p == n_pages - 1)
    def _():
        bias_ref[...] = near_bias(past_len - (n_pages - 1) * PAGE_SIZE)

    def update(k_all, v_all, mask_f, bias):
        kb = k_all.astype(BF16)
        vb = v_all.astype(BF16)
        s = jnp.concatenate([_dot_nt(q_rows(g), kb[:, g * HEAD_DIM:(g + 1) * HEAD_DIM]) for g in range(n_groups)],
                            axis=0) * ATT_SCALE + bias
        mask = jnp.concatenate([jnp.concatenate([mask_f(g)] * R, axis=0) for g in range(n_groups)], axis=0) > 0.5
        s = jnp.where(mask, s, NEG)
        m_old = m_ref[...]
        m_new = jnp.maximum(m_old, jnp.max(s, axis=-1, keepdims=True))
        alpha = jnp.exp(m_old - m_new)
        pr = jnp.where(mask, jnp.exp(s - m_new), 0.0)
        l_ref[...] = alpha * l_ref[...] + jnp.sum(pr, axis=-1, keepdims=True)
        prb = pr.astype(BF16)
        pv = jnp.concatenate([_dot(prb[g * rows_g:(g + 1) * rows_g], vb[:, g * HEAD_DIM:(g + 1) * HEAD_DIM])
                              for g in range(n_groups)], axis=0)
        acc_ref[...] = alpha * acc_ref[...] + pv
        m_ref[...] = m_new

    update(k_ref[...], v_ref[...], lambda g: mask_ref[g], bias_ref[...])

    @pl.when(p == n_pages - 1)
    def _():
        qi = lax.broadcasted_iota(jnp.int32, (t_new, TILE), 0)
        kj = lax.broadcasted_iota(jnp.int32, (t_new, TILE), 1)
        causal = (qi - kj >= 0) & (kj < t_new)
        pad = jnp.zeros((TILE - t_new, n_groups * HEAD_DIM), F32)
        k_new = jnp.concatenate([kn_ref[...], pad], axis=0)
        v_new = jnp.concatenate([vn_ref[...], pad], axis=0)
        update(k_new, v_new, lambda g: jnp.where(causal, nmask_ref[g], 0.0), near_bias(0))
        o_sel = acc_ref[...] / jnp.maximum(l_ref[...], 1e-30)
        gates = _sigmoid(gl_ref[...])
        part = part_ref[...]
        for h in range(n_heads):
            gs = gates[:, h * N_BRANCH + 1:h * N_BRANCH + 2]
            o = part[:, h * HEAD_DIM:(h + 1) * HEAD_DIM] + gs * o_sel[h * t_new:(h + 1) * t_new]
            o_ref[:, h * HEAD_DIM:(h + 1) * HEAD_DIM] = o


def nsa_sample_b(table, rel_bias, qn, gl, part, mask, nmask, cache_k, cache_v, ks_new, proj, *, n_batch, t_new,
                 n_groups, n_pages, past_len, col_vs):
    width = n_groups * HEAD_DIM
    d_q = qn.shape[1]
    rows = n_groups * NSA_GROUP * t_new
    cvs = col_vs // width
    grid_spec = pltpu.PrefetchScalarGridSpec(
        num_scalar_prefetch=1,
        grid=(n_batch, n_pages),
        in_specs=[pl.BlockSpec(memory_space=pltpu.SMEM),
                  pl.BlockSpec((t_new, d_q), lambda b, p, t: (b, 0)),
                  pl.BlockSpec((t_new, LANES), lambda b, p, t: (b, 0)),
                  pl.BlockSpec((t_new, d_q), lambda b, p, t: (b, 0)),
                  pl.BlockSpec((None, n_groups, t_new, PAGE_SIZE), lambda b, p, t: (b, 0, 0, p)),
                  pl.BlockSpec((None, n_groups, t_new, LANES), lambda b, p, t: (b, 0, 0, 0)),
                  pl.BlockSpec((PAGE_SIZE, width), lambda b, p, t: (t[b * n_pages + p], 0)),
                  pl.BlockSpec((PAGE_SIZE, width), lambda b, p, t: (t[b * n_pages + p], 0)),
                  pl.BlockSpec((t_new, width), lambda b, p, t: (b, 0)),
                  pl.BlockSpec((t_new, width), lambda b, p, t: (b, cvs))],
        out_specs=pl.BlockSpec((t_new, d_q), lambda b, p, t: (b, 0)),
        scratch_shapes=[pltpu.VMEM((rows, 1), F32), pltpu.VMEM((rows, 1), F32),
                        pltpu.VMEM((rows, HEAD_DIM), F32), pltpu.VMEM((rows, TILE), F32)],
    )
    return pl.pallas_call(
        functools.partial(_nsa_sample_b_kernel, n_pages=n_pages, n_groups=n_groups, t_new=t_new, past_len=past_len),
        grid_spec=grid_spec,
        out_shape=jax.ShapeDtypeStruct((n_batch * t_new, d_q), F32),
        compiler_params=_params("parallel", "arbitrary"),
        name="nsa_sample_b",
    )(table, rel_bias, qn, gl, part, mask, nmask, cache_k, cache_v, ks_new, proj)


def _sb_tile(z, valid, carry, upper):
    ls = _log_sigmoid(z)
    lk = ls - z
    if valid is not None:
        lk = jnp.where(valid, lk, 0.0)
    hi = lk.astype(BF16)
    lo = (lk - hi.astype(F32)).astype(BF16)
    suffix = _dot(hi, upper) + _dot(lo, upper)
    a = jnp.exp(ls + suffix + carry)
    if valid is not None:
        a = jnp.where(valid, a, 0.0)
    return a, carry + jnp.sum(lk, axis=-1, keepdims=True)


def _strict_upper():
    j = lax.broadcasted_iota(jnp.int32, (TILE, TILE), 0)
    s = lax.broadcasted_iota(jnp.int32, (TILE, TILE), 1)
    return jnp.where(j > s, 1.0, 0.0).astype(BF16)


def _sb_prompt_kernel(q_ref, k_ref, v_ref, o_ref):
    qt = pl.program_id(2)
    qb = q_ref[...].astype(BF16)
    upper = _strict_upper()
    ii = lax.broadcasted_iota(jnp.int32, (TILE, TILE), 0)
    jj = lax.broadcasted_iota(jnp.int32, (TILE, TILE), 1)

    def tile(kt, valid, carry, acc):
        start = pl.multiple_of(kt * TILE, TILE)
        k = k_ref[pl.ds(start, TILE), :].astype(BF16)
        v = v_ref[pl.ds(start, TILE), :].astype(BF16)
        z = _dot_nt(qb, k) * ATT_SCALE
        a, carry = _sb_tile(z, valid, carry, upper)
        return carry, acc + _dot(a.astype(BF16), v)

    carry, acc = tile(qt, jj < ii, jnp.zeros((TILE, 1), F32), jnp.zeros((TILE, HEAD_DIM), F32))

    def body(n, c):
        return tile(qt - 1 - n, None, *c)

    _, acc = lax.fori_loop(0, qt, body, (carry, acc))
    o_ref[...] = acc.astype(o_ref.dtype)


def sb_prompt(qkv, *, n_batch, t, n_heads):
    n_qt = t // TILE
    return pl.pallas_call(
        _sb_prompt_kernel,
        grid=(n_batch, n_heads, n_qt),
        in_specs=[pl.BlockSpec((TILE, HEAD_DIM), lambda b, h, i: (b * n_qt + i, h)),
                  pl.BlockSpec((t, HEAD_DIM), lambda b, h, i: (b, n_heads + h)),
                  pl.BlockSpec((t, HEAD_DIM), lambda b, h, i: (b, 2 * n_heads + h))],
        out_specs=pl.BlockSpec((TILE, HEAD_DIM), lambda b, h, i: (b * n_qt + i, h)),
        out_shape=jax.ShapeDtypeStruct((n_batch * t, n_heads * HEAD_DIM), BF16),
        compiler_params=_params("parallel", "parallel", "arbitrary"),
        name="sb_prompt",
    )(qkv, qkv, qkv)


def _sb_sample_kernel(tbl_ref, q_ref, kn_ref, vn_ref, k_ref, v_ref, o_ref, qs_ref, c_ref, acc_ref, *,
                      n_pages, n_heads, t_new):
    del tbl_ref
    p = pl.program_id(1)
    upper = _strict_upper()

    def tile(k_all, v_all, valid):
        kb = k_all.astype(BF16)
        vb = v_all.astype(BF16)
        qs = qs_ref[...].astype(BF16)
        z = jnp.concatenate([_dot_nt(qs[h * t_new:(h + 1) * t_new], kb[:, h * HEAD_DIM:(h + 1) * HEAD_DIM])
                             for h in range(n_heads)], axis=0) * ATT_SCALE
        a, carry = _sb_tile(z, valid, c_ref[...], upper)
        ab = a.astype(BF16)
        pv = jnp.concatenate([_dot(ab[h * t_new:(h + 1) * t_new], vb[:, h * HEAD_DIM:(h + 1) * HEAD_DIM])
                              for h in range(n_heads)], axis=0)
        c_ref[...] = carry
        acc_ref[...] += pv

    @pl.when(p == 0)
    def _():
        q = q_ref[...]
        qs_ref[...] = jnp.concatenate([q[:, h * HEAD_DIM:(h + 1) * HEAD_DIM] for h in range(n_heads)], axis=0)
        c_ref[...] = jnp.zeros(c_ref.shape, F32)
        acc_ref[...] = jnp.zeros(acc_ref.shape, F32)
        rows = n_heads * t_new
        qi = lax.broadcasted_iota(jnp.int32, (rows, TILE), 0) % t_new
        kj = lax.broadcasted_iota(jnp.int32, (rows, TILE), 1)
        pad = jnp.zeros((TILE - t_new, n_heads * HEAD_DIM), F32)
        tile(jnp.concatenate([kn_ref[...], pad], axis=0), jnp.concatenate([vn_ref[...], pad], axis=0), kj < qi)

    tile(k_ref[...], v_ref[...], None)

    @pl.when(p == n_pages - 1)
    def _():
        acc = acc_ref[...]
        for h in range(n_heads):
            o_ref[:, h * HEAD_DIM:(h + 1) * HEAD_DIM] = acc[h * t_new:(h + 1) * t_new]


def sb_sample(table, qkv, cache_k, cache_v, *, n_batch, t_new, n_heads, n_pages):
    d = n_heads * HEAD_DIM
    rows = n_heads * t_new
    grid_spec = pltpu.PrefetchScalarGridSpec(
        num_scalar_prefetch=1,
        grid=(n_batch, n_pages),
        in_specs=[pl.BlockSpec((t_new, d), lambda b, p, t: (b, 0)),
                  pl.BlockSpec((t_new, d), lambda b, p, t: (b, 1)),
                  pl.BlockSpec((t_new, d), lambda b, p, t: (b, 2)),
                  pl.BlockSpec((PAGE_SIZE, d), lambda b, p, t: (t[b * n_pages + n_pages - 1 - p], 0)),
                  pl.BlockSpec((PAGE_SIZE, d), lambda b, p, t: (t[b * n_pages + n_pages - 1 - p], 0))],
        out_specs=pl.BlockSpec((t_new, d), lambda b, p, t: (b, 0)),
        scratch_shapes=[pltpu.VMEM((rows, HEAD_DIM), F32), pltpu.VMEM((rows, 1), F32),
                        pltpu.VMEM((rows, HEAD_DIM), F32)],
    )
    return pl.pallas_call(
        functools.partial(_sb_sample_kernel, n_pages=n_pages, n_heads=n_heads, t_new=t_new),
        grid_spec=grid_spec,
        out_shape=jax.ShapeDtypeStruct((n_batch * t_new, d), F32),
        compiler_params=_params("parallel", "arbitrary"),
        name="sb_sample",
    )(table, qkv, qkv, qkv, cache_k, cache_v)


def _pick(n, candidates):
    for c in candidates:
        if n % c == 0:
            return c
    return n


def _mm(xs, w, res=None, nk=1, name="matmul"):
    m = xs[0].shape[0]
    n = w.shape[1]
    tm = _pick(m, (1024, 512, 256, 128))
    tn = _pick(n, (512, 256, 128))
    return matmul(xs, w, res, tm=tm, tn=tn, nk=nk, name=name)


def kernel(x_prompt, x_sample, state_conv_a, cache_cmp_k, cache_cmp_v, cache_sel_k, cache_sel_v, state_win_k,
           state_win_v, cache_sb_k, cache_sb_v, state_ffn_conv, page_table, norm_mix, norm_ffn, w_in_even, conv_a_w,
           q_norm, k_norm, w_cmp_k, w_cmp_v, pe_cmp, rel_bias, w_out_even, w_qkv_odd, w_out_odd, w_ffn_gate,
           w_ffn_up, ffn_conv_w, w_ffn_down):
    bp, seq, d_model = x_prompt.shape
    bs, t_new, _ = x_sample.shape
    depth = norm_mix.shape[0]
    n_pages = page_table.shape[1]
    past_len = n_pages * PAGE_SIZE
    d_conv = conv_a_w.shape[2]
    n_heads_nsa = rel_bias.shape[1]
    n_groups = n_heads_nsa // NSA_GROUP
    d_q = n_heads_nsa * HEAD_DIM
    d_kv = n_groups * HEAD_DIM
    sb_heads = w_out_odd.shape[1] // HEAD_DIM
    d_sb = sb_heads * HEAD_DIM
    d_ff = w_ffn_gate.shape[2]
    n_gate_cols = N_BRANCH * n_heads_nsa
    col_q = 3 * d_conv
    col_kv = col_q + d_q
    col_gl = col_kv + 2 * N_BRANCH * d_kv
    assert n_gate_cols <= LANES and seq % TILE == 0 and seq >= WINDOW and t_new <= CMP_BLOCK

    xp = x_prompt.reshape(bp * seq, d_model)
    xs = x_sample.reshape(bs * t_new, d_model)
    table = page_table.reshape(-1).astype(jnp.int32)
    pages_p = seq // PAGE_SIZE
    table_p = jnp.arange(bp * pages_p, dtype=jnp.int32)
    zeros_conv_p = jnp.zeros((bp, 2, d_conv), F32)
    zeros_ffn_p = jnp.zeros((bp, 2, d_ff), F32)
    dt = bias_tiles(rel_bias)

    outs = {k: [] for k in ("conv_p", "conv_s", "cmp_kp", "cmp_vp", "cmp_ks", "cmp_vs", "sel_kp", "sel_vp", "sel_ks",
                            "sel_vs", "win_kp", "win_vp", "win_ks", "win_vs", "sb_kp", "sb_vp", "sb_ks", "sb_vs",
                            "ffn_p", "ffn_s")}

    def kv4(a, b, t):
        return a.reshape(b, t, n_groups, HEAD_DIM)

    for layer in range(depth):
        hp = rmsnorm_bf16(xp, norm_mix[layer])
        hs = rmsnorm_bf16(xs, norm_mix[layer])
        if layer % 2 == 0:
            e = layer // 2
            w_main = w_in_even[e][:, :col_gl].astype(BF16)
            w_gl = jnp.pad(w_in_even[e][:, col_gl:], ((0, 0), (0, LANES - n_gate_cols))).astype(BF16)
            w_out = w_out_even[e].astype(BF16)
            wk, wv = w_cmp_k[e].astype(BF16), w_cmp_v[e].astype(BF16)
            cols = [col_kv + i * d_kv for i in range(2 * N_BRANCH)]

            proj = _mm([hp], w_main, name="in_proj_p")
            gl = _mm([hp], w_gl, name="gate_proj_p")
            y_a, hist_p = conv_gate(proj, zeros_conv_p, conv_a_w[e], n_batch=bp, t=seq, d_conv=d_conv,
                                    out_dtype=BF16)
            ks_n = headnorm(proj, cols[2], n_groups, k_norm[e, 1])
            kw_n = headnorm(proj, cols[4], n_groups, k_norm[e, 2])
            kc, vc = compress(proj, proj, cols[0], cols[1], table_p, pe_cmp[e], wk, wv, k_norm[e, 0],
                              n_batch=bp, n_pages=pages_p, chunk=pages_p, n_groups=n_groups)
            o = nsa_prompt(proj, gl, kc, vc, ks_n, kw_n, dt, rel_bias, q_norm[e], n_batch=bp, t=seq,
                           n_groups=n_groups, col_q=col_q, col_vs=cols[3], col_vw=cols[5])
            xp = _mm([y_a, o], w_out, xp, name="out_proj_p")
            outs["conv_p"].append(hist_p)
            outs["cmp_kp"].append(kv4(proj[:, cols[0]:cols[0] + d_kv], bp, seq))
            outs["cmp_vp"].append(kv4(proj[:, cols[1]:cols[1] + d_kv], bp, seq))
            outs["sel_kp"].append(kv4(ks_n, bp, seq))
            outs["sel_vp"].append(kv4(proj[:, cols[3]:cols[3] + d_kv], bp, seq))
            n_keep = min(WINDOW, seq)
            outs["win_kp"].append(kv4(kw_n, bp, seq)[:, seq - n_keep:])
            outs["win_vp"].append(kv4(proj[:, cols[5]:cols[5] + d_kv], bp, seq)[:, seq - n_keep:])

            proj = _mm([hs], w_main, name="in_proj_s")
            gl = _mm([hs], w_gl, name="gate_proj_s")
            y_a, hist_s = conv_gate(proj, state_conv_a[e], conv_a_w[e], n_batch=bs, t=t_new, d_conv=d_conv,
                                    out_dtype=F32)
            ks_n = headnorm(proj, cols[2], n_groups, k_norm[e, 1])
            kw_n = headnorm(proj, cols[4], n_groups, k_norm[e, 2])
            pool = lambda c: c[e].reshape(-1, d_kv)
            chunk = _pick(n_pages, (32, 16, 8, 4, 2, 1))
            kc, vc = compress(pool(cache_cmp_k), pool(cache_cmp_v), 0, 0, table, pe_cmp[e], wk, wv, k_norm[e, 0],
                              n_batch=bs, n_pages=n_pages, chunk=chunk, n_groups=n_groups)
            ns = -(-(past_len + t_new) // CMP_BLOCK)
            pad_rows = ((0, 0), (0, 0), (0, _round_up(ns, LANES) - kc.shape[2]), (0, 0))
            kc, vc = jnp.pad(kc, pad_rows), jnp.pad(vc, pad_rows)
            w_buf = state_win_k.shape[2]
            win_k = state_win_k[e].reshape(bs, w_buf, d_kv)
            win_v = state_win_v[e].reshape(bs, w_buf, d_kv)
            qn, part, sel = nsa_sample_a(proj, gl, kc, vc, win_k, win_v, kw_n, rel_bias, q_norm[e], n_batch=bs,
                                         t_new=t_new, n_groups=n_groups, past_len=past_len, col_q=col_q,
                                         col_vw=cols[5])
            n_past_blocks = past_len // CMP_BLOCK
            mask = jnp.repeat(sel[..., :n_past_blocks], CMP_BLOCK, axis=-1)
            nmask = jnp.broadcast_to(sel[..., n_past_blocks:n_past_blocks + 1], sel.shape[:-1] + (LANES,))
            o = nsa_sample_b(table, rel_bias, qn, gl, part, mask, nmask, pool(cache_sel_k), pool(cache_sel_v), ks_n,
                             proj, n_batch=bs, t_new=t_new, n_groups=n_groups, n_pages=n_pages, past_len=past_len,
                             col_vs=cols[3])
            xs = _mm([y_a.astype(BF16), o.astype(BF16)], w_out, xs, name="out_proj_s")
            outs["conv_s"].append(hist_s)
            outs["cmp_ks"].append(kv4(proj[:, cols[0]:cols[0] + d_kv], bs, t_new))
            outs["cmp_vs"].append(kv4(proj[:, cols[1]:cols[1] + d_kv], bs, t_new))
            outs["sel_ks"].append(kv4(ks_n, bs, t_new))
            outs["sel_vs"].append(kv4(proj[:, cols[3]:cols[3] + d_kv], bs, t_new))
            kw_all = jnp.concatenate([state_win_k[e], kv4(kw_n, bs, t_new)], axis=1)
            vw_all = jnp.concatenate([state_win_v[e], kv4(proj[:, cols[5]:cols[5] + d_kv], bs, t_new)], axis=1)
            outs["win_ks"].append(kw_all[:, kw_all.shape[1] - w_buf:])
            outs["win_vs"].append(vw_all[:, vw_all.shape[1] - w_buf:])
        else:
            o_ix = layer // 2
            w_qkv = w_qkv_odd[o_ix].astype(BF16)
            w_out = w_out_odd[o_ix].astype(BF16)
            qkv = _mm([hp], w_qkv, name="qkv_p")
            o = sb_prompt(qkv, n_batch=bp, t=seq, n_heads=sb_heads)
            xp = _mm([o], w_out, xp, name="sb_out_p")
            outs["sb_kp"].append(qkv[:, d_sb:2 * d_sb].reshape(bp, seq, sb_heads, HEAD_DIM))
            outs["sb_vp"].append(qkv[:, 2 * d_sb:].reshape(bp, seq, sb_heads, HEAD_DIM))

            qkv = _mm([hs], w_qkv, name="qkv_s")
            o = sb_sample(table, qkv, cache_sb_k[o_ix].reshape(-1, d_sb), cache_sb_v[o_ix].reshape(-1, d_sb),
                          n_batch=bs, t_new=t_new, n_heads=sb_heads, n_pages=n_pages)
            xs = _mm([o.astype(BF16)], w_out, xs, name="sb_out_s")
            outs["sb_ks"].append(qkv[:, d_sb:2 * d_sb].reshape(bs, t_new, sb_heads, HEAD_DIM))
            outs["sb_vs"].append(qkv[:, 2 * d_sb:].reshape(bs, t_new, sb_heads, HEAD_DIM))

        w_gu = jnp.concatenate([w_ffn_gate[layer].astype(BF16), w_ffn_up[layer].astype(BF16)], axis=1)
        w_down = w_ffn_down[layer].astype(BF16)
        nk_down = 2 if d_ff % (2 * LANES) == 0 else 1
        gu = _mm([rmsnorm_bf16(xp, norm_ffn[layer])], w_gu, name="ffn_gu_p")
        act, hf_p = ffn_act(gu, zeros_ffn_p, ffn_conv_w[layer], n_batch=bp, t=seq, d_ff=d_ff, out_dtype=BF16)
        xp = _mm([act], w_down, xp, nk=nk_down, name="ffn_down_p")
        gu = _mm([rmsnorm_bf16(xs, norm_ffn[layer])], w_gu, name="ffn_gu_s")
        act, hf_s = ffn_act(gu, state_ffn_conv[layer], ffn_conv_w[layer], n_batch=bs, t=t_new, d_ff=d_ff,
                            out_dtype=F32)
        xs = _mm([act.astype(BF16)], w_down, xs, nk=nk_down, name="ffn_down_s")
        outs["ffn_p"].append(hf_p)
        outs["ffn_s"].append(hf_s)

    st = lambda key: jnp.stack(outs[key], axis=0)
    return (xp.reshape(bp, seq, d_model), xs.reshape(bs, t_new, d_model), st("conv_p"), st("conv_s"),
            st("cmp_kp"), st("cmp_vp"), st("cmp_ks"), st("cmp_vs"),
            st("sel_kp"), st("sel_vp"), st("sel_ks"), st("sel_vs"),
            st("win_kp"), st("win_vp"), st("win_ks"), st("win_vs"),
            st("sb_kp"), st("sb_vp"), st("sb_ks"), st("sb_vs"),
            st("ffn_p"), st("ffn_s"))
```

```python
import functools
import math

import jax
import jax.numpy as jnp
from jax import lax
from jax.experimental import pallas as pl
from jax.experimental.pallas import tpu as pltpu

F32 = jnp.float32
BF16 = jnp.bfloat16

HEAD_DIM = 128
LANES = 128
PAGE_SIZE = 128
CMP_BLOCK = 64
TOP_N = 16
WINDOW = 512
NSA_GROUP = 4
N_BRANCH = 3
N_BUCKETS = 32
MAX_DISTANCE = 128
MAX_EXACT = N_BUCKETS // 2
FAR_BUCKET = N_BUCKETS - 1
FORCED_SCORE = float(NSA_GROUP + 1)
RMS_EPS = 1e-6
NEG = -1e30
ATT_SCALE = HEAD_DIM ** -0.5
TILE = 128
VMEM_LIMIT = 56 * 1024 * 1024


def _params(*sem):
    return pltpu.CompilerParams(dimension_semantics=sem, vmem_limit_bytes=VMEM_LIMIT)


def _dot(a, b):
    return jnp.dot(a, b, preferred_element_type=F32)


def _dot_nt(a, b):
    return lax.dot_general(a, b, (((1,), (1,)), ((), ())), preferred_element_type=F32)


def _rms(x, gain):
    return x * lax.rsqrt(jnp.mean(x * x, axis=-1, keepdims=True) + RMS_EPS) * gain


def _sigmoid(x):
    return 1.0 / (1.0 + jnp.exp(-x))


def _t5_bucket(dist):
    n = jnp.maximum(dist, 0)
    nf = jnp.maximum(n, 1).astype(F32)
    large = MAX_EXACT + (jnp.log(nf / MAX_EXACT) / math.log(MAX_DISTANCE / MAX_EXACT)
                         * (N_BUCKETS - MAX_EXACT)).astype(jnp.int32)
    large = jnp.minimum(large, N_BUCKETS - 1)
    return jnp.where(n < MAX_EXACT, n, large)


def _bias_from_bucket(bucket, rb_ref, head):
    out = jnp.zeros(bucket.shape, F32)
    for k in range(N_BUCKETS):
        out = jnp.where(bucket == k, rb_ref[k, head], out)
    return out


def _masked_softmax(l, mask):
    l = jnp.where(mask, l, NEG)
    m = jnp.max(l, axis=-1, keepdims=True)
    e = jnp.where(mask, jnp.exp(l - m), 0.0)
    return e / jnp.maximum(jnp.sum(e, axis=-1, keepdims=True), 1e-30)


def _log_sigmoid(z):
    return jnp.minimum(z, 0.0) - jnp.log(1.0 + jnp.exp(-jnp.abs(z)))


def _column(x, c):
    lane = lax.broadcasted_iota(jnp.int32, x.shape, 1)
    return jnp.sum(jnp.where(lane == c, x, 0.0), axis=1, keepdims=True)


def _round_up(n, m):
    return -(-n // m) * m


def _rmsnorm_kernel(x_ref, g_ref, o_ref):
    o_ref[...] = _rms(x_ref[...], g_ref[...]).astype(o_ref.dtype)


def rmsnorm_bf16(x, gain):
    m, d = x.shape
    tr = min(m, 256)
    return pl.pallas_call(
        _rmsnorm_kernel,
        grid=(m // tr,),
        in_specs=[pl.BlockSpec((tr, d), lambda i: (i, 0)),
                  pl.BlockSpec((1, d), lambda i: (0, 0))],
        out_specs=pl.BlockSpec((tr, d), lambda i: (i, 0)),
        out_shape=jax.ShapeDtypeStruct((m, d), BF16),
        compiler_params=_params("parallel"),
        name="rmsnorm",
    )(x, gain.reshape(1, d))


def _matmul_kernel(*refs, n_x, k_sizes, has_res, nk):
    x_refs = refs[:n_x]
    w_ref = refs[n_x]
    r_ref = refs[n_x + 1] if has_res else None
    o_ref = refs[n_x + 1 + int(has_res)]
    acc_ref = refs[-1] if nk > 1 else None

    acc = None
    k0 = 0
    for x_ref, ks in zip(x_refs, k_sizes):
        part = _dot(x_ref[...], w_ref[k0:k0 + ks, :])
        acc = part if acc is None else acc + part
        k0 += ks

    def finish(total):
        if has_res:
            total = total + r_ref[...]
        o_ref[...] = total.astype(o_ref.dtype)

    if nk == 1:
        finish(acc)
    else:
        k = pl.program_id(2)

        @pl.when(k == 0)
        def _():
            acc_ref[...] = acc

        @pl.when(k > 0)
        def _():
            acc_ref[...] += acc

        @pl.when(k == nk - 1)
        def _():
            finish(acc_ref[...])


def matmul(xs, w, res=None, *, tm, tn, nk=1, out_dtype=F32, name="matmul"):
    m = xs[0].shape[0]
    k_sizes = tuple(x.shape[1] for x in xs)
    k_total, n = w.shape
    assert sum(k_sizes) == k_total and m % tm == 0 and n % tn == 0
    assert nk == 1 or len(xs) == 1
    tk = k_total // nk
    if nk == 1:
        x_specs = [pl.BlockSpec((tm, ks), lambda i, j, k: (i, 0)) for ks in k_sizes]
        k_blk = k_sizes
    else:
        x_specs = [pl.BlockSpec((tm, tk), lambda i, j, k: (i, k))]
        k_blk = (tk,)
    in_specs = x_specs + [pl.BlockSpec((tk, tn), lambda i, j, k: (k, j))]
    args = list(xs) + [w]
    if res is not None:
        in_specs.append(pl.BlockSpec((tm, tn), lambda i, j, k: (i, j)))
        args.append(res)
    scratch = [pltpu.VMEM((tm, tn), F32)] if nk > 1 else []
    return pl.pallas_call(
        functools.partial(_matmul_kernel, n_x=len(xs), k_sizes=k_blk, has_res=res is not None, nk=nk),
        grid=(m // tm, n // tn, nk),
        in_specs=in_specs,
        out_specs=pl.BlockSpec((tm, tn), lambda i, j, k: (i, j)),
        out_shape=jax.ShapeDtypeStruct((m, n), out_dtype),
        scratch_shapes=scratch,
        compiler_params=_params("parallel", "parallel", "arbitrary"),
        name=name,
    )(*args)


def _conv3(u, hist, w):
    t = u.shape[0]
    row = lax.broadcasted_iota(jnp.int32, u.shape, 0)
    h0, h1 = hist[0:1], hist[1:2]
    u1 = jnp.where(row == 0, h1, pltpu.roll(u, 1, 0))
    u2 = jnp.where(row == 0, h0, jnp.where(row == 1, h1, pltpu.roll(u, 2, 0)))
    y = w[2:3] * u
    y = y + w[0:1] * u2
    y = y + w[1:2] * u1
    return y, u[t - 2:t]


def _conv_gate_kernel(xa_ref, ba_ref, ca_ref, hist_ref, w_ref, y_ref, nh_ref):
    u = ca_ref[...] * xa_ref[...]
    conv, new_hist = _conv3(u, hist_ref[0], w_ref[...])
    y_ref[...] = (ba_ref[...] * conv).astype(y_ref.dtype)
    nh_ref[0] = new_hist


def conv_gate(proj, hist, w, *, n_batch, t, d_conv, out_dtype, tc=256):
    nb = d_conv // tc
    return pl.pallas_call(
        _conv_gate_kernel,
        grid=(n_batch, nb),
        in_specs=[pl.BlockSpec((t, tc), lambda b, j: (b, j)),
                  pl.BlockSpec((t, tc), lambda b, j: (b, nb + j)),
                  pl.BlockSpec((t, tc), lambda b, j: (b, 2 * nb + j)),
                  pl.BlockSpec((1, 2, tc), lambda b, j: (b, 0, j)),
                  pl.BlockSpec((3, tc), lambda b, j: (0, j))],
        out_specs=[pl.BlockSpec((t, tc), lambda b, j: (b, j)),
                   pl.BlockSpec((1, 2, tc), lambda b, j: (b, 0, j))],
        out_shape=[jax.ShapeDtypeStruct((n_batch * t, d_conv), out_dtype),
                   jax.ShapeDtypeStruct((n_batch, 2, d_conv), F32)],
        compiler_params=_params("parallel", "parallel"),
        name="conv_gate",
    )(proj, proj, proj, hist, w)


def _ffn_act_kernel(g_ref, u_ref, hist_ref, w_ref, a_ref, nh_ref):
    gate = g_ref[...]
    gc, new_hist = _conv3(gate, hist_ref[0], w_ref[...])
    a_ref[...] = (gc * _sigmoid(gc) * u_ref[...]).astype(a_ref.dtype)
    nh_ref[0] = new_hist


def ffn_act(gu, hist, w, *, n_batch, t, d_ff, out_dtype, tc=256):
    nb = d_ff // tc
    return pl.pallas_call(
        _ffn_act_kernel,
        grid=(n_batch, nb),
        in_specs=[pl.BlockSpec((t, tc), lambda b, j: (b, j)),
                  pl.BlockSpec((t, tc), lambda b, j: (b, nb + j)),
                  pl.BlockSpec((1, 2, tc), lambda b, j: (b, 0, j)),
                  pl.BlockSpec((3, tc), lambda b, j: (0, j))],
        out_specs=[pl.BlockSpec((t, tc), lambda b, j: (b, j)),
                   pl.BlockSpec((1, 2, tc), lambda b, j: (b, 0, j))],
        out_shape=[jax.ShapeDtypeStruct((n_batch * t, d_ff), out_dtype),
                   jax.ShapeDtypeStruct((n_batch, 2, d_ff), F32)],
        compiler_params=_params("parallel", "parallel"),
        name="ffn_act",
    )(gu, gu, hist, w)


def _headnorm_kernel(x_ref, g_ref, o_ref):
    o_ref[...] = _rms(x_ref[...], g_ref[...])


def headnorm(src, col0, n_heads, gain):
    m = src.shape[0]
    tr = min(m, 512)
    cb = col0 // HEAD_DIM
    return pl.pallas_call(
        _headnorm_kernel,
        grid=(m // tr, n_heads),
        in_specs=[pl.BlockSpec((tr, HEAD_DIM), lambda i, h: (i, cb + h)),
                  pl.BlockSpec((1, HEAD_DIM), lambda i, h: (0, 0))],
        out_specs=pl.BlockSpec((tr, HEAD_DIM), lambda i, h: (i, h)),
        out_shape=jax.ShapeDtypeStruct((m, n_heads * HEAD_DIM), F32),
        compiler_params=_params("parallel", "parallel"),
        name="headnorm",
    )(src, gain.reshape(1, HEAD_DIM))


CMP_PARTS = 4


def _compress_kernel(tbl_ref, k_ref, v_ref, pe_ref, wk_ref, wv_ref, g_ref, kc_ref, vc_ref, ks_ref, vs_ref, *,
                     chunk, n_groups):
    del tbl_ref
    slot = pl.program_id(1) % chunk
    off = pl.multiple_of(slot * PAGE_SIZE, PAGE_SIZE)
    for g in range(n_groups):
        ks_ref[g, pl.ds(off, PAGE_SIZE), :] = k_ref[:, g * HEAD_DIM:(g + 1) * HEAD_DIM] + pe_ref[...]
        vs_ref[g, pl.ds(off, PAGE_SIZE), :] = v_ref[:, g * HEAD_DIM:(g + 1) * HEAD_DIM] + pe_ref[...]
    nb = chunk * (PAGE_SIZE // CMP_BLOCK)

    @pl.when(slot == chunk - 1)
    def _():
        for s_ref, w_ref, o_ref, norm in ((ks_ref, wk_ref, kc_ref, True), (vs_ref, wv_ref, vc_ref, False)):
            parts = [None] * CMP_PARTS
            for lp in range(CMP_BLOCK // 2):
                a = jnp.concatenate(
                    [jnp.concatenate([s_ref[g, pl.ds(2 * lp + i, nb, stride=CMP_BLOCK), :] for i in range(2)], axis=1)
                     for g in range(n_groups)], axis=0)
                d = _dot(a.astype(BF16), w_ref[lp])
                parts[lp % CMP_PARTS] = d if parts[lp % CMP_PARTS] is None else parts[lp % CMP_PARTS] + d
            acc = functools.reduce(lambda x, y: x + y, parts)
            if norm:
                acc = _rms(acc, g_ref[...])
            for g in range(n_groups):
                o_ref[0, g] = acc[g * nb:(g + 1) * nb]


def compress(src_k, src_v, col_k, col_v, table, pe, wk, wv, gain, *, n_batch, n_pages, chunk, n_groups):
    width = n_groups * HEAD_DIM
    nb_total = n_pages * (PAGE_SIZE // CMP_BLOCK)
    nb_chunk = chunk * (PAGE_SIZE // CMP_BLOCK)
    ck, cv = col_k // width, col_v // width
    pe_t = jnp.tile(pe, (PAGE_SIZE // CMP_BLOCK, 1))
    grid_spec = pltpu.PrefetchScalarGridSpec(
        num_scalar_prefetch=1,
        grid=(n_batch, n_pages),
        in_specs=[pl.BlockSpec((PAGE_SIZE, width), lambda b, p, t: (t[b * n_pages + p], ck)),
                  pl.BlockSpec((PAGE_SIZE, width), lambda b, p, t: (t[b * n_pages + p], cv)),
                  pl.BlockSpec((PAGE_SIZE, HEAD_DIM), lambda b, p, t: (0, 0)),
                  pl.BlockSpec((CMP_BLOCK // 2, 2 * HEAD_DIM, HEAD_DIM), lambda b, p, t: (0, 0, 0)),
                  pl.BlockSpec((CMP_BLOCK // 2, 2 * HEAD_DIM, HEAD_DIM), lambda b, p, t: (0, 0, 0)),
                  pl.BlockSpec((1, HEAD_DIM), lambda b, p, t: (0, 0))],
        out_specs=[pl.BlockSpec((1, n_groups, nb_chunk, HEAD_DIM), lambda b, p, t: (b, 0, p // chunk, 0)),
                   pl.BlockSpec((1, n_groups, nb_chunk, HEAD_DIM), lambda b, p, t: (b, 0, p // chunk, 0))],
        scratch_shapes=[pltpu.VMEM((n_groups, chunk * PAGE_SIZE, HEAD_DIM), F32),
                        pltpu.VMEM((n_groups, chunk * PAGE_SIZE, HEAD_DIM), F32)],
    )
    out = jax.ShapeDtypeStruct((n_batch, n_groups, nb_total, HEAD_DIM), F32)
    return pl.pallas_call(
        functools.partial(_compress_kernel, chunk=chunk, n_groups=n_groups),
        grid_spec=grid_spec,
        out_shape=[out, out],
        compiler_params=_params("parallel", "arbitrary"),
        name="compress",
    )(table, src_k, src_v, pe_t, wk.reshape(CMP_BLOCK // 2, 2 * HEAD_DIM, HEAD_DIM),
      wv.reshape(CMP_BLOCK // 2, 2 * HEAD_DIM, HEAD_DIM), gain.reshape(1, HEAD_DIM))


def _bias_tiles_kernel(rb_ref, o_ref):
    h = pl.program_id(0)
    i = lax.broadcasted_iota(jnp.int32, (TILE, TILE), 0)
    j = lax.broadcasted_iota(jnp.int32, (TILE, TILE), 1)
    for m in range(2):
        o_ref[0, m] = _bias_from_bucket(_t5_bucket(i - j + TILE * m), rb_ref, h)


def bias_tiles(rel_bias):
    n_heads = rel_bias.shape[1]
    return pl.pallas_call(
        _bias_tiles_kernel,
        grid=(n_heads,),
        in_specs=[pl.BlockSpec(memory_space=pltpu.SMEM)],
        out_specs=pl.BlockSpec((1, 2, TILE, TILE), lambda h: (h, 0, 0, 0)),
        out_shape=jax.ShapeDtypeStruct((n_heads, 2, TILE, TILE), F32),
        compiler_params=_params("arbitrary"),
        name="bias_tiles",
    )(rel_bias)


def _select_blocks(imp, q_pos, ns):
    blk = lax.broadcasted_iota(jnp.int32, imp.shape, 1)
    cur = q_pos // CMP_BLOCK
    forced = (blk == 0) | (blk == cur) | (blk == cur - 1)
    score = jnp.where(forced, FORCED_SCORE, jnp.where(blk <= cur, imp, -1.0))
    cnt = jnp.zeros(imp.shape, jnp.int32)
    for i in range(ns):
        ci = score[:, i:i + 1]
        ahead = (ci > score) | ((ci == score) & (blk > i))
        cnt = cnt + jnp.where(ahead, 1, 0)
    return jnp.where((cnt < min(TOP_N, ns)) & (blk < ns), 1.0, 0.0)


def _nsa_prompt_kernel(rb_ref, q_ref, qg_ref, gl_ref, kc_ref, vc_ref, ks_ref, vs_ref, kw_ref, vw_ref, d_ref,
                       o_ref, selt_ref, *, n_cmp, n_kt):
    g = pl.program_id(1)
    qt = pl.program_id(2)
    R = NSA_GROUP
    q = q_ref[...]
    qn = jnp.concatenate([_rms(q[:, r * HEAD_DIM:(r + 1) * HEAD_DIM], qg_ref[...]) for r in range(R)], axis=0)
    qb = qn.astype(BF16)
    gates = _sigmoid(gl_ref[...])
    ii = lax.broadcasted_iota(jnp.int32, (TILE, TILE), 0)
    jj = lax.broadcasted_iota(jnp.int32, (TILE, TILE), 1)
    far = [rb_ref[FAR_BUCKET, g * R + r] for r in range(R)]

    q_pos = qt * TILE + lax.broadcasted_iota(jnp.int32, (TILE, 1), 0)
    c_end = (lax.broadcasted_iota(jnp.int32, (1, n_cmp), 1) + 1) * CMP_BLOCK - 1
    c_dist = q_pos - c_end
    c_bucket = _t5_bucket(c_dist)
    lc = _dot_nt(qb, kc_ref[...].astype(BF16)) * ATT_SCALE
    vcb = vc_ref[...].astype(BF16)
    imp = jnp.zeros((TILE, n_cmp), F32)
    o_cmp = []
    for r in range(R):
        l_r = lc[r * TILE:(r + 1) * TILE] + _bias_from_bucket(c_bucket, rb_ref, g * R + r)
        p_r = _masked_softmax(l_r, c_dist >= 0)
        imp = imp + p_r
        o_cmp.append(_dot(p_r.astype(BF16), vcb))

    selb = _select_blocks(imp, q_pos, n_cmp).astype(BF16)
    bn = lax.broadcasted_iota(jnp.int32, (n_cmp, TILE), 0)
    bj = lax.broadcasted_iota(jnp.int32, (n_cmp, TILE), 1)
    for kt in range(n_kt):
        expand = jnp.where(bn == kt * (TILE // CMP_BLOCK) + bj // CMP_BLOCK, 1.0, 0.0).astype(BF16)
        selt_ref[kt] = _dot(selb, expand)

    def tile_bias(r, rel):
        if isinstance(rel, int):
            return d_ref[r, rel] if rel < 2 else far[r]
        return jnp.where(rel == 0, d_ref[r, 0], jnp.where(rel == 1, d_ref[r, 1], far[r]))

    def sel_body(kt, carry):
        ms, ls, accs = carry
        rel = qt - kt
        start = pl.multiple_of(kt * TILE, TILE)
        k = ks_ref[pl.ds(start, TILE), :].astype(BF16)
        v = vs_ref[pl.ds(start, TILE), :].astype(BF16)
        s = _dot_nt(qb, k) * ATT_SCALE
        mask = (selt_ref[kt] > 0.5) & (ii - jj + TILE * rel >= 0)
        new_m, new_l, new_acc = [], [], []
        for r in range(R):
            s_r = jnp.where(mask, s[r * TILE:(r + 1) * TILE] + tile_bias(r, rel), NEG)
            m_new = jnp.maximum(ms[r], jnp.max(s_r, axis=-1, keepdims=True))
            alpha = jnp.exp(ms[r] - m_new)
            p = jnp.where(mask, jnp.exp(s_r - m_new), 0.0)
            new_m.append(m_new)
            new_l.append(alpha * ls[r] + jnp.sum(p, axis=-1, keepdims=True))
            new_acc.append(alpha * accs[r] + _dot(p.astype(BF16), v))
        return tuple(new_m), tuple(new_l), tuple(new_acc)

    init = (tuple(jnp.full((TILE, 1), NEG, F32) for _ in range(R)),
            tuple(jnp.zeros((TILE, 1), F32) for _ in range(R)),
            tuple(jnp.zeros((TILE, HEAD_DIM), F32) for _ in range(R)))
    _, l_sel, acc_sel = lax.fori_loop(0, qt + 1, sel_body, init)

    n_rel = WINDOW // TILE + 1
    s_tiles, v_tiles, masks = [], [], []
    for rel in range(n_rel):
        kt = qt - rel
        start = pl.multiple_of(jnp.maximum(kt, 0) * TILE, TILE)
        k = kw_ref[pl.ds(start, TILE), :].astype(BF16)
        v_tiles.append(vw_ref[pl.ds(start, TILE), :].astype(BF16))
        s_tiles.append(_dot_nt(qb, k) * ATT_SCALE)
        dist = ii - jj + TILE * rel + jnp.where(kt >= 0, 0, WINDOW)
        masks.append((dist >= 0) & (dist < WINDOW))
    vwin = jnp.concatenate(v_tiles, axis=0)
    wmask = jnp.concatenate([jnp.where(m, 1.0, 0.0) for m in masks], axis=1) > 0.5

    for r in range(R):
        lw = jnp.concatenate([s_tiles[rel][r * TILE:(r + 1) * TILE] + tile_bias(r, rel)
                              for rel in range(n_rel)], axis=1)
        pw = _masked_softmax(lw, wmask)
        o_win = _dot(pw.astype(BF16), vwin)
        o_sel = acc_sel[r] / jnp.maximum(l_sel[r], 1e-30)
        c = (g * R + r) * N_BRANCH
        o = _column(gates, c) * o_cmp[r] + _column(gates, c + 1) * o_sel + _column(gates, c + 2) * o_win
        o_ref[:, r * HEAD_DIM:(r + 1) * HEAD_DIM] = o.astype(o_ref.dtype)


def nsa_prompt(proj, gl, kc, vc, ks, kw, dt, rel_bias, q_gain, *, n_batch, t, n_groups, col_q, col_vs, col_vw):
    n_qt = t // TILE
    n_cmp = kc.shape[2]
    gw = NSA_GROUP * HEAD_DIM
    cq, cvs, cvw = col_q // gw, col_vs // HEAD_DIM, col_vw // HEAD_DIM
    return pl.pallas_call(
        functools.partial(_nsa_prompt_kernel, n_cmp=n_cmp, n_kt=n_qt),
        grid=(n_batch, n_groups, n_qt),
        in_specs=[pl.BlockSpec(memory_space=pltpu.SMEM),
                  pl.BlockSpec((TILE, gw), lambda b, g, i: (b * n_qt + i, cq + g)),
                  pl.BlockSpec((1, HEAD_DIM), lambda b, g, i: (0, 0)),
                  pl.BlockSpec((TILE, LANES), lambda b, g, i: (b * n_qt + i, 0)),
                  pl.BlockSpec((None, None, n_cmp, HEAD_DIM), lambda b, g, i: (b, g, 0, 0)),
                  pl.BlockSpec((None, None, n_cmp, HEAD_DIM), lambda b, g, i: (b, g, 0, 0)),
                  pl.BlockSpec((t, HEAD_DIM), lambda b, g, i: (b, g)),
                  pl.BlockSpec((t, HEAD_DIM), lambda b, g, i: (b, cvs + g)),
                  pl.BlockSpec((t, HEAD_DIM), lambda b, g, i: (b, g)),
                  pl.BlockSpec((t, HEAD_DIM), lambda b, g, i: (b, cvw + g)),
                  pl.BlockSpec((NSA_GROUP, 2, TILE, TILE), lambda b, g, i: (g, 0, 0, 0))],
        out_specs=pl.BlockSpec((TILE, gw), lambda b, g, i: (b * n_qt + i, g)),
        out_shape=jax.ShapeDtypeStruct((n_batch * t, n_groups * gw), BF16),
        scratch_shapes=[pltpu.VMEM((n_qt, TILE, TILE), F32)],
        compiler_params=_params("parallel", "parallel", "arbitrary"),
        name="nsa_prompt",
    )(rel_bias, proj, q_gain.reshape(1, HEAD_DIM), gl, kc, vc, ks, proj, kw, proj, dt)


def _nsa_sample_a_kernel(rb_ref, q_ref, qg_ref, gl_ref, kc_ref, vc_ref, wk_ref, wv_ref, kn_ref, vn_ref,
                         qn_ref, part_ref, sel_ref, *, ns, t_new, past_len, w_buf):
    g = pl.program_id(1)
    R = NSA_GROUP
    n_cols = kc_ref.shape[0]
    q = q_ref[...]
    qn = jnp.concatenate([_rms(q[:, r * HEAD_DIM:(r + 1) * HEAD_DIM], qg_ref[...]) for r in range(R)], axis=0)
    qb = qn.astype(BF16)
    gates = _sigmoid(gl_ref[...])
    q_pos = past_len + lax.broadcasted_iota(jnp.int32, (t_new, 1), 0)

    c_end = (lax.broadcasted_iota(jnp.int32, (1, n_cols), 1) + 1) * CMP_BLOCK - 1
    c_dist = q_pos - c_end
    c_bucket = _t5_bucket(c_dist)
    lc = _dot_nt(qb, kc_ref[...].astype(BF16)) * ATT_SCALE
    vcb = vc_ref[...].astype(BF16)
    imp = jnp.zeros((t_new, n_cols), F32)
    o_cmp = []
    for r in range(R):
        l_r = lc[r * t_new:(r + 1) * t_new] + _bias_from_bucket(c_bucket, rb_ref, g * R + r)
        p_r = _masked_softmax(l_r, c_dist >= 0)
        imp = imp + p_r
        o_cmp.append(_dot(p_r.astype(BF16), vcb))

    sel_ref[...] = _select_blocks(imp, q_pos, ns)

    pad = jnp.zeros((TILE - t_new, HEAD_DIM), F32)
    kwin = jnp.concatenate([wk_ref[...], kn_ref[...], pad], axis=0).astype(BF16)
    vwin = jnp.concatenate([wv_ref[...], vn_ref[...], pad], axis=0).astype(BF16)
    nw = w_buf + TILE
    col = lax.broadcasted_iota(jnp.int32, (1, nw), 1)
    w_pos = past_len - w_buf + col
    w_dist = q_pos - w_pos
    w_mask = (w_dist >= 0) & (w_dist < WINDOW) & (w_pos >= 0) & (col < w_buf + t_new)
    w_bucket = _t5_bucket(w_dist)
    lw = _dot_nt(qb, kwin) * ATT_SCALE
    for r in range(R):
        l_r = lw[r * t_new:(r + 1) * t_new] + _bias_from_bucket(w_bucket, rb_ref, g * R + r)
        o_win = _dot(_masked_softmax(l_r, w_mask).astype(BF16), vwin)
        c = (g * R + r) * N_BRANCH
        part_ref[:, r * HEAD_DIM:(r + 1) * HEAD_DIM] = _column(gates, c) * o_cmp[r] + _column(gates, c + 2) * o_win
        qn_ref[:, r * HEAD_DIM:(r + 1) * HEAD_DIM] = qn[r * t_new:(r + 1) * t_new]


def nsa_sample_a(proj, gl, kc, vc, win_k, win_v, kw_new, rel_bias, q_gain, *, n_batch, t_new, n_groups, past_len,
                 col_q, col_vw):
    ns_pad = kc.shape[2]
    ns = -(-(past_len + t_new) // CMP_BLOCK)
    w_buf = win_k.shape[1]
    gw = NSA_GROUP * HEAD_DIM
    cq, cvw = col_q // gw, col_vw // HEAD_DIM
    d_q = n_groups * gw
    return pl.pallas_call(
        functools.partial(_nsa_sample_a_kernel, ns=ns, t_new=t_new, past_len=past_len, w_buf=w_buf),
        grid=(n_batch, n_groups),
        in_specs=[pl.BlockSpec(memory_space=pltpu.SMEM),
                  pl.BlockSpec((t_new, gw), lambda b, g: (b, cq + g)),
                  pl.BlockSpec((1, HEAD_DIM), lambda b, g: (0, 0)),
                  pl.BlockSpec((t_new, LANES), lambda b, g: (b, 0)),
                  pl.BlockSpec((None, None, ns_pad, HEAD_DIM), lambda b, g: (b, g, 0, 0)),
                  pl.BlockSpec((None, None, ns_pad, HEAD_DIM), lambda b, g: (b, g, 0, 0)),
                  pl.BlockSpec((None, w_buf, HEAD_DIM), lambda b, g: (b, 0, g)),
                  pl.BlockSpec((None, w_buf, HEAD_DIM), lambda b, g: (b, 0, g)),
                  pl.BlockSpec((t_new, HEAD_DIM), lambda b, g: (b, g)),
                  pl.BlockSpec((t_new, HEAD_DIM), lambda b, g: (b, cvw + g))],
        out_specs=[pl.BlockSpec((t_new, gw), lambda b, g: (b, g)),
                   pl.BlockSpec((t_new, gw), lambda b, g: (b, g)),
                   pl.BlockSpec((None, None, t_new, ns_pad), lambda b, g: (b, g, 0, 0))],
        out_shape=[jax.ShapeDtypeStruct((n_batch * t_new, d_q), F32),
                   jax.ShapeDtypeStruct((n_batch * t_new, d_q), F32),
                   jax.ShapeDtypeStruct((n_batch, n_groups, t_new, ns_pad), F32)],
        compiler_params=_params("parallel", "parallel"),
        name="nsa_sample_a",
    )(rel_bias, proj, q_gain.reshape(1, HEAD_DIM), gl, kc, vc, win_k, win_v, kw_new, proj)


def _nsa_sample_b_kernel(tbl_ref, rb_ref, qn_ref, gl_ref, part_ref, mask_ref, nmask_ref, k_ref, v_ref, kn_ref, vn_ref,
                         o_ref, m_ref, l_ref, acc_ref, bias_ref, *, n_pages, n_groups, t_new, past_len):
    del tbl_ref
    p = pl.program_id(1)
    R = NSA_GROUP
    rows_g = R * t_new
    rows = n_groups * rows_g
    row_head = lax.broadcasted_iota(jnp.int32, (rows, 1), 0) // t_new
    n_heads = n_groups * R

    def q_rows(g):
        qn = qn_ref[...]
        return jnp.concatenate([qn[:, (g * R + r) * HEAD_DIM:(g * R + r + 1) * HEAD_DIM] for r in range(R)],
                               axis=0).astype(BF16)

    def near_bias(base):
        qi = lax.broadcasted_iota(jnp.int32, (t_new, TILE), 0)
        kj = lax.broadcasted_iota(jnp.int32, (t_new, TILE), 1)
        bucket = _t5_bucket(qi - kj + base)
        return jnp.concatenate([_bias_from_bucket(bucket, rb_ref, h) for h in range(n_heads)], axis=0)

    @pl.when(p == 0)
    def _():
        m_ref[...] = jnp.full(m_ref.shape, NEG, F32)
        l_ref[...] = jnp.zeros(l_ref.shape, F32)
        acc_ref[...] = jnp.zeros(acc_ref.shape, F32)
        far = jnp.zeros((rows, 1), F32)
        for h in range(n_heads):
            far = jnp.where(row_head == h, rb_ref[FAR_BUCKET, h], far)
        bias_ref[...] = jnp.broadcast_to(far, bias_ref.shape)

    @pl.when(p == n_pages - 1)
    def _():
        bias_ref[...] = near_bias(past_len - (n_pages - 1) * PAGE_SIZE)

    def update(k_all, v_all, mask_f, bias):
        kb = k_all.astype(BF16)
        vb = v_all.astype(BF16)
        s = jnp.concatenate([_dot_nt(q_rows(g), kb[:, g * HEAD_DIM:(g + 1) * HEAD_DIM]) for g in range(n_groups)],
                            axis=0) * ATT_SCALE + bias
        mask = jnp.concatenate([jnp.concatenate([mask_f(g)] * R, axis=0) for g in range(n_groups)], axis=0) > 0.5
        s = jnp.where(mask, s, NEG)
        m_old = m_ref[...]
        m_new = jnp.maximum(m_old, jnp.max(s, axis=-1, keepdims=True))
        alpha = jnp.exp(m_old - m_new)
        pr = jnp.where(mask, jnp.exp(s - m_new), 0.0)
        l_ref[...] = alpha * l_ref[...] + jnp.sum(pr, axis=-1, keepdims=True)
        prb = pr.astype(BF16)
        pv = jnp.concatenate([_dot(prb[g * rows_g:(g + 1) * rows_g], vb[:, g * HEAD_DIM:(g + 1) * HEAD_DIM])
                              for g in range(n_groups)], axis=0)
        acc_ref[...] = alpha * acc_ref[...] + pv
        m_ref[...] = m_new

    update(k_ref[...], v_ref[...], lambda g: mask_ref[g], bias_ref[...])

    @pl.when(p == n_pages - 1)
    def _():
        qi = lax.broadcasted_iota(jnp.int32, (t_new, TILE), 0)
        kj = lax.broadcasted_iota(jnp.int32, (t_new, TILE), 1)
        causal = (qi - kj >= 0) & (kj < t_new)
        pad = jnp.zeros((TILE - t_new, n_groups * HEAD_DIM), F32)
        k_new = jnp.concatenate([kn_ref[...], pad], axis=0)
        v_new = jnp.concatenate([vn_ref[...], pad], axis=0)
        update(k_new, v_new, lambda g: jnp.where(causal, nmask_ref[g], 0.0), near_bias(0))
        o_sel = acc_ref[...] / jnp.maximum(l_ref[...], 1e-30)
        gates = _sigmoid(gl_ref[...])
        part = part_ref[...]
        for h in range(n_heads):
            gs = gates[:, h * N_BRANCH + 1:h * N_BRANCH + 2]
            o = part[:, h * HEAD_DIM:(h + 1) * HEAD_DIM] + gs * o_sel[h * t_new:(h + 1) * t_new]
            o_ref[:, h * HEAD_DIM:(h + 1) * HEAD_DIM] = o


def nsa_sample_b(table, rel_bias, qn, gl, part, mask, nmask, cache_k, cache_v, ks_new, proj, *, n_batch, t_new,
                 n_groups, n_pages, past_len, col_vs):
    width = n_groups * HEAD_DIM
    d_q = qn.shape[1]
    rows = n_groups * NSA_GROUP * t_new
    cvs = col_vs // width
    grid_spec = pltpu.PrefetchScalarGridSpec(
        num_scalar_prefetch=1,
        grid=(n_batch, n_pages),
        in_specs=[pl.BlockSpec(memory_space=pltpu.SMEM),
                  pl.BlockSpec((t_new, d_q), lambda b, p, t: (b, 0)),
                  pl.BlockSpec((t_new, LANES), lambda b, p, t: (b, 0)),
                  pl.BlockSpec((t_new, d_q), lambda b, p, t: (b, 0)),
                  pl.BlockSpec((None, n_groups, t_new, PAGE_SIZE), lambda b, p, t: (b, 0, 0, p)),
                  pl.BlockSpec((None, n_groups, t_new, LANES), lambda b, p, t: (b, 0, 0, 0)),
                  pl.BlockSpec((PAGE_SIZE, width), lambda b, p, t: (t[b * n_pages + p], 0)),
                  pl.BlockSpec((PAGE_SIZE, width), lambda b, p, t: (t[b * n_pages + p], 0)),
                  pl.BlockSpec((t_new, width), lambda b, p, t: (b, 0)),
                  pl.BlockSpec((t_new, width), lambda b, p, t: (b, cvs))],
        out_specs=pl.BlockSpec((t_new, d_q), lambda b, p, t: (b, 0)),
        scratch_shapes=[pltpu.VMEM((rows, 1), F32), pltpu.VMEM((rows, 1), F32),
                        pltpu.VMEM((rows, HEAD_DIM), F32), pltpu.VMEM((rows, TILE), F32)],
    )
    return pl.pallas_call(
        functools.partial(_nsa_sample_b_kernel, n_pages=n_pages, n_groups=n_groups, t_new=t_new, past_len=past_len),
        grid_spec=grid_spec,
        out_shape=jax.ShapeDtypeStruct((n_batch * t_new, d_q), F32),
        compiler_params=_params("parallel", "arbitrary"),
        name="nsa_sample_b",
    )(table, rel_bias, qn, gl, part, mask, nmask, cache_k, cache_v, ks_new, proj)


def _sb_tile(z, valid, carry, upper):
    rows = z.shape[0]
    w = upper.shape[0]
    n = z.shape[1] // w
    ls = _log_sigmoid(z)
    lk = ls - z
    if valid is not None:
        lk = jnp.where(valid, lk, 0.0)
    chunks = [None] * n
    for c in reversed(range(n)):
        lkc = lk[:, c * w:(c + 1) * w]
        hi = lkc.astype(BF16)
        lo = (lkc - hi.astype(F32)).astype(BF16)
        both = _dot(jnp.concatenate([hi, lo], axis=0), upper)
        a = jnp.exp(ls[:, c * w:(c + 1) * w] + (both[:rows] + both[rows:]) + carry)
        if valid is not None:
            a = jnp.where(valid[:, c * w:(c + 1) * w], a, 0.0)
        chunks[c] = a
        carry = carry + jnp.sum(lkc, axis=-1, keepdims=True)
    return (chunks[0] if n == 1 else jnp.concatenate(chunks, axis=1)), carry


def _strict_upper(n):
    j = lax.broadcasted_iota(jnp.int32, (n, n), 0)
    s = lax.broadcasted_iota(jnp.int32, (n, n), 1)
    return jnp.where(j > s, 1.0, 0.0).astype(BF16)


SB_TQ = 512
SB_TK = 256


def _sb_prompt_kernel(q_ref, k_ref, v_ref, o_ref, *, tq, tk):
    qi = pl.program_id(2)
    qb = q_ref[...].astype(BF16)
    upper = _strict_upper(tk)
    n_diag = tq // tk

    def tile(start, q_rows, valid, carry, acc):
        k = k_ref[pl.ds(start, tk), :].astype(BF16)
        v = v_ref[pl.ds(start, tk), :].astype(BF16)
        z = _dot_nt(q_rows, k) * ATT_SCALE
        a, carry = _sb_tile(z, valid, carry, upper)
        return carry, acc + _dot(a.astype(BF16), v)

    carry = jnp.zeros((tq, 1), F32)
    acc = jnp.zeros((tq, HEAD_DIM), F32)
    for d in reversed(range(n_diag)):
        r0 = d * tk
        ii = lax.broadcasted_iota(jnp.int32, (tq - r0, tk), 0)
        jj = lax.broadcasted_iota(jnp.int32, (tq - r0, tk), 1)
        start = pl.multiple_of(qi * tq + r0, tk)
        c_new, a_new = tile(start, qb[r0:], jj < ii, carry[r0:], acc[r0:])
        carry = c_new if r0 == 0 else jnp.concatenate([carry[:r0], c_new], axis=0)
        acc = a_new if r0 == 0 else jnp.concatenate([acc[:r0], a_new], axis=0)

    def body(n, c):
        start = pl.multiple_of((qi * n_diag - 1 - n) * tk, tk)
        return tile(start, qb, None, *c)

    _, acc = lax.fori_loop(0, qi * n_diag, body, (carry, acc))
    o_ref[...] = acc.astype(o_ref.dtype)


def sb_prompt(qkv, *, n_batch, t, n_heads):
    tq = _pick(t, (SB_TQ, SB_TK))
    tk = SB_TK
    n_qt = t // tq
    return pl.pallas_call(
        functools.partial(_sb_prompt_kernel, tq=tq, tk=tk),
        grid=(n_batch, n_heads, n_qt),
        in_specs=[pl.BlockSpec((tq, HEAD_DIM), lambda b, h, i: (b * n_qt + i, h)),
                  pl.BlockSpec((t, HEAD_DIM), lambda b, h, i: (b, n_heads + h)),
                  pl.BlockSpec((t, HEAD_DIM), lambda b, h, i: (b, 2 * n_heads + h))],
        out_specs=pl.BlockSpec((tq, HEAD_DIM), lambda b, h, i: (b * n_qt + i, h)),
        out_shape=jax.ShapeDtypeStruct((n_batch * t, n_heads * HEAD_DIM), BF16),
        compiler_params=_params("parallel", "parallel", "arbitrary"),
        name="sb_prompt",
    )(qkv, qkv, qkv)


SUBLANES = 8


def _sb_sample_kernel(tbl_ref, q_ref, kn_ref, vn_ref, k_ref, v_ref, o_ref, qs_ref, c_ref, acc_ref, *,
                      n_pages, n_heads, t_new):
    del tbl_ref
    p = pl.program_id(1)
    upper = _strict_upper(SB_TK)
    hb_rows = SUBLANES * t_new
    n_lanes = PAGE_SIZE * SUBLANES

    @pl.when(p == 0)
    def _():
        q = q_ref[...]
        qs = jnp.concatenate([q[:, h * HEAD_DIM:(h + 1) * HEAD_DIM] for h in range(n_heads)], axis=0)
        qs_ref[...] = qs
        rows = n_heads * t_new
        qi = lax.broadcasted_iota(jnp.int32, (rows, TILE), 0) % t_new
        kj = lax.broadcasted_iota(jnp.int32, (rows, TILE), 1)
        pad = jnp.zeros((TILE - t_new, n_heads * HEAD_DIM), F32)
        kb = jnp.concatenate([kn_ref[...], pad], axis=0).astype(BF16)
        vb = jnp.concatenate([vn_ref[...], pad], axis=0).astype(BF16)
        qsb = qs.astype(BF16)
        z = jnp.concatenate([_dot_nt(qsb[h * t_new:(h + 1) * t_new], kb[:, h * HEAD_DIM:(h + 1) * HEAD_DIM])
                             for h in range(n_heads)], axis=0) * ATT_SCALE
        a, carry = _sb_tile(z, kj < qi, jnp.zeros((rows, 1), F32), upper[:TILE, :TILE])
        ab = a.astype(BF16)
        acc_ref[...] = jnp.concatenate([_dot(ab[h * t_new:(h + 1) * t_new], vb[:, h * HEAD_DIM:(h + 1) * HEAD_DIM])
                                        for h in range(n_heads)], axis=0)
        c_ref[...] = carry

    lane = lax.broadcasted_iota(jnp.int32, (hb_rows, n_lanes), 1)
    row = lax.broadcasted_iota(jnp.int32, (hb_rows, n_lanes), 0)
    own = lane % SUBLANES == row // t_new
    for hb in range(n_heads // SUBLANES):
        rs = slice(hb * hb_rows, (hb + 1) * hb_rows)
        kb = k_ref[:, hb * SUBLANES:(hb + 1) * SUBLANES, :].reshape(n_lanes, HEAD_DIM).astype(BF16)
        vb = v_ref[:, hb * SUBLANES:(hb + 1) * SUBLANES, :].reshape(n_lanes, HEAD_DIM).astype(BF16)
        z = _dot_nt(qs_ref[rs, :].astype(BF16), kb) * ATT_SCALE
        a, carry = _sb_tile(z, own, c_ref[rs, :], upper)
        acc_ref[rs, :] += _dot(a.astype(BF16), vb)
        c_ref[rs, :] = carry

    @pl.when(p == n_pages - 1)
    def _():
        acc = acc_ref[...]
        for h in range(n_heads):
            o_ref[:, h * HEAD_DIM:(h + 1) * HEAD_DIM] = acc[h * t_new:(h + 1) * t_new]


def sb_sample(table, qkv, cache_k, cache_v, layer, *, n_batch, t_new, n_heads, n_pages):
    d = n_heads * HEAD_DIM
    rows = n_heads * t_new
    assert n_heads % SUBLANES == 0 and (PAGE_SIZE * SUBLANES) % SB_TK == 0
    page = lambda b, p, t: (layer, t[b * n_pages + n_pages - 1 - p], 0, 0, 0)
    grid_spec = pltpu.PrefetchScalarGridSpec(
        num_scalar_prefetch=1,
        grid=(n_batch, n_pages),
        in_specs=[pl.BlockSpec((t_new, d), lambda b, p, t: (b, 0)),
                  pl.BlockSpec((t_new, d), lambda b, p, t: (b, 1)),
                  pl.BlockSpec((t_new, d), lambda b, p, t: (b, 2)),
                  pl.BlockSpec((None, None, PAGE_SIZE, n_heads, HEAD_DIM), page),
                  pl.BlockSpec((None, None, PAGE_SIZE, n_heads, HEAD_DIM), page)],
        out_specs=pl.BlockSpec((t_new, d), lambda b, p, t: (b, 0)),
        scratch_shapes=[pltpu.VMEM((rows, HEAD_DIM), F32), pltpu.VMEM((rows, 1), F32),
                        pltpu.VMEM((rows, HEAD_DIM), F32)],
    )
    return pl.pallas_call(
        functools.partial(_sb_sample_kernel, n_pages=n_pages, n_heads=n_heads, t_new=t_new),
        grid_spec=grid_spec,
        out_shape=jax.ShapeDtypeStruct((n_batch * t_new, d), F32),
        compiler_params=_params("parallel", "arbitrary"),
        name="sb_sample",
    )(table, qkv, qkv, qkv, cache_k, cache_v)


def _pick(n, candidates):
    for c in candidates:
        if n % c == 0:
            return c
    return n


def _mm(xs, w, res=None, nk=1, name="matmul"):
    m = xs[0].shape[0]
    n = w.shape[1]
    tm = _pick(m, (1024, 512, 256, 128))
    tn = _pick(n, (512, 256, 128))
    return matmul(xs, w, res, tm=tm, tn=tn, nk=nk, name=name)


def kernel(x_prompt, x_sample, state_conv_a, cache_cmp_k, cache_cmp_v, cache_sel_k, cache_sel_v, state_win_k,
           state_win_v, cache_sb_k, cache_sb_v, state_ffn_conv, page_table, norm_mix, norm_ffn, w_in_even, conv_a_w,
           q_norm, k_norm, w_cmp_k, w_cmp_v, pe_cmp, rel_bias, w_out_even, w_qkv_odd, w_out_odd, w_ffn_gate,
           w_ffn_up, ffn_conv_w, w_ffn_down):
    bp, seq, d_model = x_prompt.shape
    bs, t_new, _ = x_sample.shape
    depth = norm_mix.shape[0]
    n_pages = page_table.shape[1]
    past_len = n_pages * PAGE_SIZE
    d_conv = conv_a_w.shape[2]
    n_heads_nsa = rel_bias.shape[1]
    n_groups = n_heads_nsa // NSA_GROUP
    d_q = n_heads_nsa * HEAD_DIM
    d_kv = n_groups * HEAD_DIM
    sb_heads = w_out_odd.shape[1] // HEAD_DIM
    d_sb = sb_heads * HEAD_DIM
    d_ff = w_ffn_gate.shape[2]
    n_gate_cols = N_BRANCH * n_heads_nsa
    col_q = 3 * d_conv
    col_kv = col_q + d_q
    col_gl = col_kv + 2 * N_BRANCH * d_kv
    assert n_gate_cols <= LANES and seq % TILE == 0 and seq >= WINDOW and t_new <= CMP_BLOCK

    xp = x_prompt.reshape(bp * seq, d_model)
    xs = x_sample.reshape(bs * t_new, d_model)
    table = page_table.reshape(-1).astype(jnp.int32)
    pages_p = seq // PAGE_SIZE
    table_p = jnp.arange(bp * pages_p, dtype=jnp.int32)
    zeros_conv_p = jnp.zeros((bp, 2, d_conv), F32)
    zeros_ffn_p = jnp.zeros((bp, 2, d_ff), F32)
    dt = bias_tiles(rel_bias)

    outs = {k: [] for k in ("conv_p", "conv_s", "cmp_kp", "cmp_vp", "cmp_ks", "cmp_vs", "sel_kp", "sel_vp", "sel_ks",
                            "sel_vs", "win_kp", "win_vp", "win_ks", "win_vs", "sb_kp", "sb_vp", "sb_ks", "sb_vs",
                            "ffn_p", "ffn_s")}

    def kv4(a, b, t):
        return a.reshape(b, t, n_groups, HEAD_DIM)

    for layer in range(depth):
        hp = rmsnorm_bf16(xp, norm_mix[layer])
        hs = rmsnorm_bf16(xs, norm_mix[layer])
        if layer % 2 == 0:
            e = layer // 2
            w_main = w_in_even[e][:, :col_gl].astype(BF16)
            w_gl = jnp.pad(w_in_even[e][:, col_gl:], ((0, 0), (0, LANES - n_gate_cols))).astype(BF16)
            w_out = w_out_even[e].astype(BF16)
            wk, wv = w_cmp_k[e].astype(BF16), w_cmp_v[e].astype(BF16)
            cols = [col_kv + i * d_kv for i in range(2 * N_BRANCH)]

            proj = _mm([hp], w_main, name="in_proj_p")
            gl = _mm([hp], w_gl, name="gate_proj_p")
            y_a, hist_p = conv_gate(proj, zeros_conv_p, conv_a_w[e], n_batch=bp, t=seq, d_conv=d_conv,
                                    out_dtype=BF16)
            ks_n = headnorm(proj, cols[2], n_groups, k_norm[e, 1])
            kw_n = headnorm(proj, cols[4], n_groups, k_norm[e, 2])
            kc, vc = compress(proj, proj, cols[0], cols[1], table_p, pe_cmp[e], wk, wv, k_norm[e, 0],
                              n_batch=bp, n_pages=pages_p, chunk=pages_p, n_groups=n_groups)
            o = nsa_prompt(proj, gl, kc, vc, ks_n, kw_n, dt, rel_bias, q_norm[e], n_batch=bp, t=seq,
                           n_groups=n_groups, col_q=col_q, col_vs=cols[3], col_vw=cols[5])
            xp = _mm([y_a, o], w_out, xp, name="out_proj_p")
            outs["conv_p"].append(hist_p)
            outs["cmp_kp"].append(kv4(proj[:, cols[0]:cols[0] + d_kv], bp, seq))
            outs["cmp_vp"].append(kv4(proj[:, cols[1]:cols[1] + d_kv], bp, seq))
            outs["sel_kp"].append(kv4(ks_n, bp, seq))
            outs["sel_vp"].append(kv4(proj[:, cols[3]:cols[3] + d_kv], bp, seq))
            n_keep = min(WINDOW, seq)
            outs["win_kp"].append(kv4(kw_n, bp, seq)[:, seq - n_keep:])
            outs["win_vp"].append(kv4(proj[:, cols[5]:cols[5] + d_kv], bp, seq)[:, seq - n_keep:])

            proj = _mm([hs], w_main, name="in_proj_s")
            gl = _mm([hs], w_gl, name="gate_proj_s")
            y_a, hist_s = conv_gate(proj, state_conv_a[e], conv_a_w[e], n_batch=bs, t=t_new, d_conv=d_conv,
                                    out_dtype=F32)
            ks_n = headnorm(proj, cols[2], n_groups, k_norm[e, 1])
            kw_n = headnorm(proj, cols[4], n_groups, k_norm[e, 2])
            pool = lambda c: c[e].reshape(-1, d_kv)
            chunk = _pick(n_pages, (32, 16, 8, 4, 2, 1))
            kc, vc = compress(pool(cache_cmp_k), pool(cache_cmp_v), 0, 0, table, pe_cmp[e], wk, wv, k_norm[e, 0],
                              n_batch=bs, n_pages=n_pages, chunk=chunk, n_groups=n_groups)
            ns = -(-(past_len + t_new) // CMP_BLOCK)
            pad_rows = ((0, 0), (0, 0), (0, _round_up(ns, LANES) - kc.shape[2]), (0, 0))
            kc, vc = jnp.pad(kc, pad_rows), jnp.pad(vc, pad_rows)
            w_buf = state_win_k.shape[2]
            win_k = state_win_k[e].reshape(bs, w_buf, d_kv)
            win_v = state_win_v[e].reshape(bs, w_buf, d_kv)
            qn, part, sel = nsa_sample_a(proj, gl, kc, vc, win_k, win_v, kw_n, rel_bias, q_norm[e], n_batch=bs,
                                         t_new=t_new, n_groups=n_groups, past_len=past_len, col_q=col_q,
                                         col_vw=cols[5])
            n_past_blocks = past_len // CMP_BLOCK
            mask = jnp.repeat(sel[..., :n_past_blocks], CMP_BLOCK, axis=-1)
            nmask = jnp.broadcast_to(sel[..., n_past_blocks:n_past_blocks + 1], sel.shape[:-1] + (LANES,))
            o = nsa_sample_b(table, rel_bias, qn, gl, part, mask, nmask, pool(cache_sel_k), pool(cache_sel_v), ks_n,
                             proj, n_batch=bs, t_new=t_new, n_groups=n_groups, n_pages=n_pages, past_len=past_len,
                             col_vs=cols[3])
            xs = _mm([y_a.astype(BF16), o.astype(BF16)], w_out, xs, name="out_proj_s")
            outs["conv_s"].append(hist_s)
            outs["cmp_ks"].append(kv4(proj[:, cols[0]:cols[0] + d_kv], bs, t_new))
            outs["cmp_vs"].append(kv4(proj[:, cols[1]:cols[1] + d_kv], bs, t_new))
            outs["sel_ks"].append(kv4(ks_n, bs, t_new))
            outs["sel_vs"].append(kv4(proj[:, cols[3]:cols[3] + d_kv], bs, t_new))
            kw_all = jnp.concatenate([state_win_k[e], kv4(kw_n, bs, t_new)], axis=1)
            vw_all = jnp.concatenate([state_win_v[e], kv4(proj[:, cols[5]:cols[5] + d_kv], bs, t_new)], axis=1)
            outs["win_ks"].append(kw_all[:, kw_all.shape[1] - w_buf:])
            outs["win_vs"].append(vw_all[:, vw_all.shape[1] - w_buf:])
        else:
            o_ix = layer // 2
            w_qkv = w_qkv_odd[o_ix].astype(BF16)
            w_out = w_out_odd[o_ix].astype(BF16)
            qkv = _mm([hp], w_qkv, name="qkv_p")
            o = sb_prompt(qkv, n_batch=bp, t=seq, n_heads=sb_heads)
            xp = _mm([o], w_out, xp, name="sb_out_p")
            outs["sb_kp"].append(qkv[:, d_sb:2 * d_sb].reshape(bp, seq, sb_heads, HEAD_DIM))
            outs["sb_vp"].append(qkv[:, 2 * d_sb:].reshape(bp, seq, sb_heads, HEAD_DIM))

            qkv = _mm([hs], w_qkv, name="qkv_s")
            o = sb_sample(table, qkv, cache_sb_k, cache_sb_v, o_ix, n_batch=bs, t_new=t_new, n_heads=sb_heads,
                          n_pages=n_pages)
            xs = _mm([o.astype(BF16)], w_out, xs, name="sb_out_s")
            outs["sb_ks"].append(qkv[:, d_sb:2 * d_sb].reshape(bs, t_new, sb_heads, HEAD_DIM))
            outs["sb_vs"].append(qkv[:, 2 * d_sb:].reshape(bs, t_new, sb_heads, HEAD_DIM))

        w_gu = jnp.concatenate([w_ffn_gate[layer].astype(BF16), w_ffn_up[layer].astype(BF16)], axis=1)
        w_down = w_ffn_down[layer].astype(BF16)
        nk_down = 2 if d_ff % (2 * LANES) == 0 else 1
        gu = _mm([rmsnorm_bf16(xp, norm_ffn[layer])], w_gu, name="ffn_gu_p")
        act, hf_p = ffn_act(gu, zeros_ffn_p, ffn_conv_w[layer], n_batch=bp, t=seq, d_ff=d_ff, out_dtype=BF16)
        xp = _mm([act], w_down, xp, nk=nk_down, name="ffn_down_p")
        gu = _mm([rmsnorm_bf16(xs, norm_ffn[layer])], w_gu, name="ffn_gu_s")
        act, hf_s = ffn_act(gu, state_ffn_conv[layer], ffn_conv_w[layer], n_batch=bs, t=t_new, d_ff=d_ff,
                            out_dtype=F32)
        xs = _mm([act.astype(BF16)], w_down, xs, nk=nk_down, name="ffn_down_s")
        outs["ffn_p"].append(hf_p)
        outs["ffn_s"].append(hf_s)

    st = lambda key: jnp.stack(outs[key], axis=0)
    return (xp.reshape(bp, seq, d_model), xs.reshape(bs, t_new, d_model), st("conv_p"), st("conv_s"),
            st("cmp_kp"), st("cmp_vp"), st("cmp_ks"), st("cmp_vs"),
            st("sel_kp"), st("sel_vp"), st("sel_ks"), st("sel_vs"),
            st("win_kp"), st("win_vp"), st("win_ks"), st("win_vs"),
            st("sb_kp"), st("sb_vp"), st("sb_ks"), st("sb_vs"),
            st("ffn_p"), st("ffn_s"))
```

```python
import functools
import math

import jax
import jax.numpy as jnp
from jax import lax
from jax.experimental import pallas as pl
from jax.experimental.pallas import tpu as pltpu

F32 = jnp.float32
BF16 = jnp.bfloat16

HEAD_DIM = 128
LANES = 128
PAGE_SIZE = 128
CMP_BLOCK = 64
TOP_N = 16
WINDOW = 512
NSA_GROUP = 4
N_BRANCH = 3
N_BUCKETS = 32
MAX_DISTANCE = 128
MAX_EXACT = N_BUCKETS // 2
FAR_BUCKET = N_BUCKETS - 1
FORCED_SCORE = float(NSA_GROUP + 1)
RMS_EPS = 1e-6
NEG = -1e30
ATT_SCALE = HEAD_DIM ** -0.5
TILE = 128
CONV_TILE_ELEMS = 512 * 1024
VMEM_LIMIT = 56 * 1024 * 1024


def _params(*sem):
    return pltpu.CompilerParams(dimension_semantics=sem, vmem_limit_bytes=VMEM_LIMIT)


def _dot(a, b):
    return jnp.dot(a, b, preferred_element_type=F32)


def _dot_nt(a, b):
    return lax.dot_general(a, b, (((1,), (1,)), ((), ())), preferred_element_type=F32)


def _rms(x, gain):
    return x * lax.rsqrt(jnp.mean(x * x, axis=-1, keepdims=True) + RMS_EPS) * gain


def _sigmoid(x):
    return 1.0 / (1.0 + jnp.exp(-x))


def _t5_bucket(dist):
    n = jnp.maximum(dist, 0)
    nf = jnp.maximum(n, 1).astype(F32)
    large = MAX_EXACT + (jnp.log(nf / MAX_EXACT) / math.log(MAX_DISTANCE / MAX_EXACT)
                         * (N_BUCKETS - MAX_EXACT)).astype(jnp.int32)
    large = jnp.minimum(large, N_BUCKETS - 1)
    return jnp.where(n < MAX_EXACT, n, large)


def _bias_from_bucket(bucket, rb_ref, head):
    out = jnp.zeros(bucket.shape, F32)
    for k in range(N_BUCKETS):
        out = jnp.where(bucket == k, rb_ref[k, head], out)
    return out


def _masked_softmax(l, mask):
    l = jnp.where(mask, l, NEG)
    m = jnp.max(l, axis=-1, keepdims=True)
    e = jnp.where(mask, jnp.exp(l - m), 0.0)
    return e / jnp.maximum(jnp.sum(e, axis=-1, keepdims=True), 1e-30)


def _log_sigmoid(z):
    return jnp.minimum(z, 0.0) - jnp.log(1.0 + jnp.exp(-jnp.abs(z)))


def _column(x, c):
    lane = lax.broadcasted_iota(jnp.int32, x.shape, 1)
    return jnp.sum(jnp.where(lane == c, x, 0.0), axis=1, keepdims=True)


def _round_up(n, m):
    return -(-n // m) * m


def _rmsnorm_kernel(x_ref, g_ref, o_ref):
    o_ref[...] = _rms(x_ref[...], g_ref[...]).astype(o_ref.dtype)


def rmsnorm_bf16(x, gain):
    m, d = x.shape
    tr = min(m, 256)
    return pl.pallas_call(
        _rmsnorm_kernel,
        grid=(m // tr,),
        in_specs=[pl.BlockSpec((tr, d), lambda i: (i, 0)),
                  pl.BlockSpec((1, d), lambda i: (0, 0))],
        out_specs=pl.BlockSpec((tr, d), lambda i: (i, 0)),
        out_shape=jax.ShapeDtypeStruct((m, d), BF16),
        compiler_params=_params("parallel"),
        name="rmsnorm",
    )(x, gain.reshape(1, d))


def _matmul_kernel(*refs, n_x, k_sizes, has_res, nk):
    x_refs = refs[:n_x]
    w_ref = refs[n_x]
    r_ref = refs[n_x + 1] if has_res else None
    o_ref = refs[n_x + 1 + int(has_res)]
    acc_ref = refs[-1] if nk > 1 else None

    acc = None
    k0 = 0
    for x_ref, ks in zip(x_refs, k_sizes):
        part = _dot(x_ref[...], w_ref[k0:k0 + ks, :])
        acc = part if acc is None else acc + part
        k0 += ks

    def finish(total):
        if has_res:
            total = total + r_ref[...]
        o_ref[...] = total.astype(o_ref.dtype)

    if nk == 1:
        finish(acc)
    else:
        k = pl.program_id(2)

        @pl.when(k == 0)
        def _():
            acc_ref[...] = acc

        @pl.when(k > 0)
        def _():
            acc_ref[...] += acc

        @pl.when(k == nk - 1)
        def _():
            finish(acc_ref[...])


def matmul(xs, w, res=None, *, tm, tn, nk=1, layer=0, col_off=0, n_cols=None, out_dtype=F32, name="matmul"):
    m = xs[0].shape[0]
    k_sizes = tuple(x.shape[1] for x in xs)
    _, k_total, n_w = w.shape
    n = n_w - col_off if n_cols is None else n_cols
    assert sum(k_sizes) == k_total and m % tm == 0 and n % tn == 0 and col_off % tn == 0
    assert nk == 1 or len(xs) == 1
    tk = k_total // nk
    jb = col_off // tn
    if nk == 1:
        x_specs = [pl.BlockSpec((tm, ks), lambda i, j, k: (i, 0)) for ks in k_sizes]
        k_blk = k_sizes
    else:
        x_specs = [pl.BlockSpec((tm, tk), lambda i, j, k: (i, k))]
        k_blk = (tk,)
    in_specs = x_specs + [pl.BlockSpec((None, tk, tn), lambda i, j, k: (layer, k, jb + j))]
    args = list(xs) + [w]
    if res is not None:
        in_specs.append(pl.BlockSpec((tm, tn), lambda i, j, k: (i, j)))
        args.append(res)
    scratch = [pltpu.VMEM((tm, tn), F32)] if nk > 1 else []
    return pl.pallas_call(
        functools.partial(_matmul_kernel, n_x=len(xs), k_sizes=k_blk, has_res=res is not None, nk=nk),
        grid=(m // tm, n // tn, nk),
        in_specs=in_specs,
        out_specs=pl.BlockSpec((tm, tn), lambda i, j, k: (i, j)),
        out_shape=jax.ShapeDtypeStruct((m, n), out_dtype),
        scratch_shapes=scratch,
        compiler_params=_params("parallel", "parallel", "arbitrary"),
        name=name,
    )(*args)


def _matmul_pair_kernel(x_ref, w1_ref, w2_ref, o1_ref, o2_ref):
    x = x_ref[...]
    o1_ref[...] = _dot(x, w1_ref[...])
    o2_ref[...] = _dot(x, w2_ref[...])


def matmul_pair(x, w1, w2, layer, *, tm, tn, name):
    m, k = x.shape
    n = w1.shape[2]
    assert w1.shape == w2.shape and w1.shape[1:] == (k, n) and m % tm == 0 and n % tn == 0
    w_spec = pl.BlockSpec((None, k, tn), lambda i, j: (layer, 0, j))
    o_spec = pl.BlockSpec((tm, tn), lambda i, j: (i, j))
    out = jax.ShapeDtypeStruct((m, n), F32)
    return pl.pallas_call(
        _matmul_pair_kernel,
        grid=(m // tm, n // tn),
        in_specs=[pl.BlockSpec((tm, k), lambda i, j: (i, 0)), w_spec, w_spec],
        out_specs=[o_spec, o_spec],
        out_shape=[out, out],
        compiler_params=_params("parallel", "parallel"),
        name=name,
    )(x, w1, w2)


def _conv3(u, hist, w):
    t = u.shape[0]
    row = lax.broadcasted_iota(jnp.int32, u.shape, 0)
    h0, h1 = hist[0:1], hist[1:2]
    u1 = jnp.where(row == 0, h1, pltpu.roll(u, 1, 0))
    u2 = jnp.where(row == 0, h0, jnp.where(row == 1, h1, pltpu.roll(u, 2, 0)))
    y = w[2:3] * u
    y = y + w[0:1] * u2
    y = y + w[1:2] * u1
    return y, u[t - 2:t]


def _conv_gate_kernel(xa_ref, ba_ref, ca_ref, hist_ref, w_ref, y_ref, nh_ref):
    u = ca_ref[...] * xa_ref[...]
    conv, new_hist = _conv3(u, hist_ref[0], w_ref[...])
    y_ref[...] = (ba_ref[...] * conv).astype(y_ref.dtype)
    nh_ref[0] = new_hist


def _conv_cols(t, width):
    cols = min(width, max(LANES, CONV_TILE_ELEMS // t // LANES * LANES))
    while width % cols:
        cols -= LANES
    return cols


def conv_gate(proj, hist, w, *, n_batch, t, d_conv, out_dtype):
    tc = _conv_cols(t, d_conv)
    nb = d_conv // tc
    return pl.pallas_call(
        _conv_gate_kernel,
        grid=(n_batch, nb),
        in_specs=[pl.BlockSpec((t, tc), lambda b, j: (b, j)),
                  pl.BlockSpec((t, tc), lambda b, j: (b, nb + j)),
                  pl.BlockSpec((t, tc), lambda b, j: (b, 2 * nb + j)),
                  pl.BlockSpec((1, 2, tc), lambda b, j: (b, 0, j)),
                  pl.BlockSpec((3, tc), lambda b, j: (0, j))],
        out_specs=[pl.BlockSpec((t, tc), lambda b, j: (b, j)),
                   pl.BlockSpec((1, 2, tc), lambda b, j: (b, 0, j))],
        out_shape=[jax.ShapeDtypeStruct((n_batch * t, d_conv), out_dtype),
                   jax.ShapeDtypeStruct((n_batch, 2, d_conv), F32)],
        compiler_params=_params("parallel", "parallel"),
        name="conv_gate",
    )(proj, proj, proj, hist, w)


def _ffn_act_kernel(g_ref, u_ref, hist_ref, w_ref, a_ref, nh_ref):
    gate = g_ref[...]
    gc, new_hist = _conv3(gate, hist_ref[0], w_ref[...])
    a_ref[...] = (gc * _sigmoid(gc) * u_ref[...]).astype(a_ref.dtype)
    nh_ref[0] = new_hist


def ffn_act(gate, up, hist, w, *, n_batch, t, d_ff, out_dtype):
    tc = _conv_cols(t, d_ff)
    nb = d_ff // tc
    return pl.pallas_call(
        _ffn_act_kernel,
        grid=(n_batch, nb),
        in_specs=[pl.BlockSpec((t, tc), lambda b, j: (b, j)),
                  pl.BlockSpec((t, tc), lambda b, j: (b, j)),
                  pl.BlockSpec((1, 2, tc), lambda b, j: (b, 0, j)),
                  pl.BlockSpec((3, tc), lambda b, j: (0, j))],
        out_specs=[pl.BlockSpec((t, tc), lambda b, j: (b, j)),
                   pl.BlockSpec((1, 2, tc), lambda b, j: (b, 0, j))],
        out_shape=[jax.ShapeDtypeStruct((n_batch * t, d_ff), out_dtype),
                   jax.ShapeDtypeStruct((n_batch, 2, d_ff), F32)],
        compiler_params=_params("parallel", "parallel"),
        name="ffn_act",
    )(gate, up, hist, w)


def _headnorm_kernel(x_ref, g_ref, o_ref):
    o_ref[...] = _rms(x_ref[...], g_ref[...])


def headnorm(src, col0, n_heads, gain):
    m = src.shape[0]
    tr = min(m, 512)
    cb = col0 // HEAD_DIM
    return pl.pallas_call(
        _headnorm_kernel,
        grid=(m // tr, n_heads),
        in_specs=[pl.BlockSpec((tr, HEAD_DIM), lambda i, h: (i, cb + h)),
                  pl.BlockSpec((1, HEAD_DIM), lambda i, h: (0, 0))],
        out_specs=pl.BlockSpec((tr, HEAD_DIM), lambda i, h: (i, h)),
        out_shape=jax.ShapeDtypeStruct((m, n_heads * HEAD_DIM), F32),
        compiler_params=_params("parallel", "parallel"),
        name="headnorm",
    )(src, gain.reshape(1, HEAD_DIM))


CMP_PARTS = 4
PAGES_PER_STEP = 4


def _page_group(page_ref, g, n_groups, by_rows):
    if by_rows:
        return page_ref[pl.ds(g, PAGE_SIZE, stride=n_groups), :]
    return page_ref[:, g * HEAD_DIM:(g + 1) * HEAD_DIM]


def _compress_kernel(*refs, chunk, n_groups, pps, by_rows):
    k_refs, v_refs = refs[1:1 + pps], refs[1 + pps:1 + 2 * pps]
    pe_ref, wk_ref, wv_ref, g_ref, kc_ref, vc_ref, ks_ref, vs_ref = refs[1 + 2 * pps:]
    steps = chunk // pps
    slot = pl.program_id(1) % steps
    for i in range(pps):
        off = pl.multiple_of((slot * pps + i) * PAGE_SIZE, PAGE_SIZE)
        for g in range(n_groups):
            ks_ref[g, pl.ds(off, PAGE_SIZE), :] = _page_group(k_refs[i], g, n_groups, by_rows) + pe_ref[...]
            vs_ref[g, pl.ds(off, PAGE_SIZE), :] = _page_group(v_refs[i], g, n_groups, by_rows) + pe_ref[...]
    nb = chunk * (PAGE_SIZE // CMP_BLOCK)

    @pl.when(slot == steps - 1)
    def _():
        for s_ref, w_ref, o_ref, norm in ((ks_ref, wk_ref, kc_ref, True), (vs_ref, wv_ref, vc_ref, False)):
            parts = [None] * CMP_PARTS
            for lp in range(CMP_BLOCK // 2):
                a = jnp.concatenate(
                    [jnp.concatenate([s_ref[g, pl.ds(2 * lp + i, nb, stride=CMP_BLOCK), :] for i in range(2)], axis=1)
                     for g in range(n_groups)], axis=0)
                d = _dot(a.astype(BF16), w_ref[lp])
                parts[lp % CMP_PARTS] = d if parts[lp % CMP_PARTS] is None else parts[lp % CMP_PARTS] + d
            acc = functools.reduce(lambda x, y: x + y, parts)
            if norm:
                acc = _rms(acc, g_ref[...])
            for g in range(n_groups):
                o_ref[0, g] = acc[g * nb:(g + 1) * nb]


def compress(src_k, src_v, col_k, col_v, table, pe, wk, wv, gain, *, n_batch, n_pages, chunk, n_groups, by_rows):
    width = n_groups * HEAD_DIM
    nb_total = n_pages * (PAGE_SIZE // CMP_BLOCK)
    nb_chunk = chunk * (PAGE_SIZE // CMP_BLOCK)
    ck, cv = col_k // width, col_v // width
    pe_t = jnp.tile(pe, (PAGE_SIZE // CMP_BLOCK, 1))
    pps = _pick(chunk, (PAGES_PER_STEP, 2, 1))
    steps = chunk // pps
    page_shape = (PAGE_SIZE * n_groups, HEAD_DIM) if by_rows else (PAGE_SIZE, width)

    def page_spec(i, col):
        return pl.BlockSpec(page_shape, lambda b, s, t: (t[b * n_pages + s * pps + i], col))

    out_spec = pl.BlockSpec((1, n_groups, nb_chunk, HEAD_DIM), lambda b, s, t: (b, 0, s // steps, 0))
    w_spec = pl.BlockSpec((CMP_BLOCK // 2, 2 * HEAD_DIM, HEAD_DIM), lambda b, s, t: (0, 0, 0))
    grid_spec = pltpu.PrefetchScalarGridSpec(
        num_scalar_prefetch=1,
        grid=(n_batch, n_pages // pps),
        in_specs=[page_spec(i, ck) for i in range(pps)] + [page_spec(i, cv) for i in range(pps)]
        + [pl.BlockSpec((PAGE_SIZE, HEAD_DIM), lambda b, s, t: (0, 0)), w_spec, w_spec,
           pl.BlockSpec((1, HEAD_DIM), lambda b, s, t: (0, 0))],
        out_specs=[out_spec, out_spec],
        scratch_shapes=[pltpu.VMEM((n_groups, chunk * PAGE_SIZE, HEAD_DIM), F32),
                        pltpu.VMEM((n_groups, chunk * PAGE_SIZE, HEAD_DIM), F32)],
    )
    out = jax.ShapeDtypeStruct((n_batch, n_groups, nb_total, HEAD_DIM), F32)
    return pl.pallas_call(
        functools.partial(_compress_kernel, chunk=chunk, n_groups=n_groups, pps=pps, by_rows=by_rows),
        grid_spec=grid_spec,
        out_shape=[out, out],
        compiler_params=_params("parallel", "arbitrary"),
        name="compress",
    )(table, *([src_k] * pps), *([src_v] * pps), pe_t, wk.reshape(CMP_BLOCK // 2, 2 * HEAD_DIM, HEAD_DIM),
      wv.reshape(CMP_BLOCK // 2, 2 * HEAD_DIM, HEAD_DIM), gain.reshape(1, HEAD_DIM))


N_BIAS_TILES = 3
CMP_NEAR = -(-(MAX_DISTANCE + CMP_BLOCK - 1) // CMP_BLOCK)


def _bias_tiles_kernel(rb_ref, o_ref):
    h = pl.program_id(0)
    i = lax.broadcasted_iota(jnp.int32, (TILE, TILE), 0)
    j = lax.broadcasted_iota(jnp.int32, (TILE, TILE), 1)
    for m in range(2):
        o_ref[0, m] = _bias_from_bucket(_t5_bucket(i - j + TILE * m), rb_ref, h)
    o_ref[0, 2] = _bias_from_bucket(_t5_bucket(i - (CMP_BLOCK - 1) + CMP_BLOCK * (j - 1)), rb_ref, h)


def bias_tiles(rel_bias):
    n_heads = rel_bias.shape[1]
    return pl.pallas_call(
        _bias_tiles_kernel,
        grid=(n_heads,),
        in_specs=[pl.BlockSpec(memory_space=pltpu.SMEM)],
        out_specs=pl.BlockSpec((1, N_BIAS_TILES, TILE, TILE), lambda h: (h, 0, 0, 0)),
        out_shape=jax.ShapeDtypeStruct((n_heads, N_BIAS_TILES, TILE, TILE), F32),
        compiler_params=_params("arbitrary"),
        name="bias_tiles",
    )(rel_bias)


def _select_blocks(imp, q_pos, ns):
    blk = lax.broadcasted_iota(jnp.int32, imp.shape, 1)
    cur = q_pos // CMP_BLOCK
    forced = (blk == 0) | (blk == cur) | (blk == cur - 1)
    score = jnp.where(forced, FORCED_SCORE, jnp.where(blk <= cur, imp, -1.0))
    cnt = jnp.zeros(imp.shape, jnp.int32)
    for i in range(ns):
        ci = score[:, i:i + 1]
        ahead = (ci > score) | ((ci == score) & (blk > i))
        cnt = cnt + jnp.where(ahead, 1, 0)
    return jnp.where((cnt < min(TOP_N, ns)) & (blk < ns), 1.0, 0.0)


def _nsa_prompt_kernel(rb_ref, q_ref, qg_ref, gl_ref, kc_ref, vc_ref, ks_ref, vs_ref, kw_ref, vw_ref, d_ref,
                       o_ref, selt_ref, *, n_cmp, n_kt):
    g = pl.program_id(1)
    qt = pl.program_id(2)
    R = NSA_GROUP
    q = q_ref[...]
    qn = jnp.concatenate([_rms(q[:, r * HEAD_DIM:(r + 1) * HEAD_DIM], qg_ref[...]) for r in range(R)], axis=0)
    qb = qn.astype(BF16)
    gates = _sigmoid(gl_ref[...])
    ii = lax.broadcasted_iota(jnp.int32, (TILE, TILE), 0)
    jj = lax.broadcasted_iota(jnp.int32, (TILE, TILE), 1)
    far = [rb_ref[FAR_BUCKET, g * R + r] for r in range(R)]

    q_pos = qt * TILE + lax.broadcasted_iota(jnp.int32, (TILE, 1), 0)
    c_end = (lax.broadcasted_iota(jnp.int32, (1, n_cmp), 1) + 1) * CMP_BLOCK - 1
    c_dist = q_pos - c_end
    c_blk = lax.broadcasted_iota(jnp.int32, (TILE, n_cmp), 1)
    lc = _dot_nt(qb, kc_ref[...].astype(BF16)) * ATT_SCALE
    vcb = vc_ref[...].astype(BF16)
    imp = jnp.zeros((TILE, n_cmp), F32)
    o_cmp = []
    for r in range(R):
        bias = jnp.full((TILE, n_cmp), far[r], F32)
        for m in range(-1, CMP_NEAR):
            bias = jnp.where(c_blk == qt * (TILE // CMP_BLOCK) - m, d_ref[r, 2][:, m + 1:m + 2], bias)
        l_r = lc[r * TILE:(r + 1) * TILE] + bias
        p_r = _masked_softmax(l_r, c_dist >= 0)
        imp = imp + p_r
        o_cmp.append(_dot(p_r.astype(BF16), vcb))

    selb = _select_blocks(imp, q_pos, n_cmp).astype(BF16)
    bn = lax.broadcasted_iota(jnp.int32, (n_cmp, TILE), 0)
    bj = lax.broadcasted_iota(jnp.int32, (n_cmp, TILE), 1)
    for kt in range(n_kt):
        expand = jnp.where(bn == kt * (TILE // CMP_BLOCK) + bj // CMP_BLOCK, 1.0, 0.0).astype(BF16)
        selt_ref[kt] = _dot(selb, expand)

    def tile_bias(r, rel):
        if isinstance(rel, int):
            return d_ref[r, rel] if rel < 2 else far[r]
        return jnp.where(rel == 0, d_ref[r, 0], jnp.where(rel == 1, d_ref[r, 1], far[r]))

    def sel_body(kt, carry):
        ms, ls, accs = carry
        rel = qt - kt
        start = pl.multiple_of(kt * TILE, TILE)
        k = ks_ref[pl.ds(start, TILE), :].astype(BF16)
        v = vs_ref[pl.ds(start, TILE), :].astype(BF16)
        s = _dot_nt(qb, k) * ATT_SCALE
        mask = (selt_ref[kt] > 0.5) & (ii - jj + TILE * rel >= 0)
        new_m, new_l, new_acc = [], [], []
        for r in range(R):
            s_r = jnp.where(mask, s[r * TILE:(r + 1) * TILE] + tile_bias(r, rel), NEG)
            m_new = jnp.maximum(ms[r], jnp.max(s_r, axis=-1, keepdims=True))
            alpha = jnp.exp(ms[r] - m_new)
            p = jnp.where(mask, jnp.exp(s_r - m_new), 0.0)
            new_m.append(m_new)
            new_l.append(alpha * ls[r] + jnp.sum(p, axis=-1, keepdims=True))
            new_acc.append(alpha * accs[r] + _dot(p.astype(BF16), v))
        return tuple(new_m), tuple(new_l), tuple(new_acc)

    init = (tuple(jnp.full((TILE, 1), NEG, F32) for _ in range(R)),
            tuple(jnp.zeros((TILE, 1), F32) for _ in range(R)),
            tuple(jnp.zeros((TILE, HEAD_DIM), F32) for _ in range(R)))
    _, l_sel, acc_sel = lax.fori_loop(0, qt + 1, sel_body, init)

    n_rel = WINDOW // TILE + 1
    s_tiles, v_tiles, masks = [], [], []
    for rel in range(n_rel):
        kt = qt - rel
        start = pl.multiple_of(jnp.maximum(kt, 0) * TILE, TILE)
        k = kw_ref[pl.ds(start, TILE), :].astype(BF16)
        v_tiles.append(vw_ref[pl.ds(start, TILE), :].astype(BF16))
        s_tiles.append(_dot_nt(qb, k) * ATT_SCALE)
        dist = ii - jj + TILE * rel + jnp.where(kt >= 0, 0, WINDOW)
        masks.append((dist >= 0) & (dist < WINDOW))
    vwin = jnp.concatenate(v_tiles, axis=0)
    wmask = jnp.concatenate([jnp.where(m, 1.0, 0.0) for m in masks], axis=1) > 0.5

    for r in range(R):
        lw = jnp.concatenate([s_tiles[rel][r * TILE:(r + 1) * TILE] + tile_bias(r, rel)
                              for rel in range(n_rel)], axis=1)
        pw = _masked_softmax(lw, wmask)
        o_win = _dot(pw.astype(BF16), vwin)
        o_sel = acc_sel[r] / jnp.maximum(l_sel[r], 1e-30)
        c = (g * R + r) * N_BRANCH
        o = _column(gates, c) * o_cmp[r] + _column(gates, c + 1) * o_sel + _column(gates, c + 2) * o_win
        o_ref[:, r * HEAD_DIM:(r + 1) * HEAD_DIM] = o.astype(o_ref.dtype)


def nsa_prompt(proj, gl, kc, vc, ks, kw, dt, rel_bias, q_gain, *, n_batch, t, n_groups, col_q, col_vs, col_vw):
    n_qt = t // TILE
    n_cmp = kc.shape[2]
    gw = NSA_GROUP * HEAD_DIM
    cq, cvs, cvw = col_q // gw, col_vs // HEAD_DIM, col_vw // HEAD_DIM
    return pl.pallas_call(
        functools.partial(_nsa_prompt_kernel, n_cmp=n_cmp, n_kt=n_qt),
        grid=(n_batch, n_groups, n_qt),
        in_specs=[pl.BlockSpec(memory_space=pltpu.SMEM),
                  pl.BlockSpec((TILE, gw), lambda b, g, i: (b * n_qt + i, cq + g)),
                  pl.BlockSpec((1, HEAD_DIM), lambda b, g, i: (0, 0)),
                  pl.BlockSpec((TILE, LANES), lambda b, g, i: (b * n_qt + i, 0)),
                  pl.BlockSpec((None, None, n_cmp, HEAD_DIM), lambda b, g, i: (b, g, 0, 0)),
                  pl.BlockSpec((None, None, n_cmp, HEAD_DIM), lambda b, g, i: (b, g, 0, 0)),
                  pl.BlockSpec((t, HEAD_DIM), lambda b, g, i: (b, g)),
                  pl.BlockSpec((t, HEAD_DIM), lambda b, g, i: (b, cvs + g)),
                  pl.BlockSpec((t, HEAD_DIM), lambda b, g, i: (b, g)),
                  pl.BlockSpec((t, HEAD_DIM), lambda b, g, i: (b, cvw + g)),
                  pl.BlockSpec((NSA_GROUP, N_BIAS_TILES, TILE, TILE), lambda b, g, i: (g, 0, 0, 0))],
        out_specs=pl.BlockSpec((TILE, gw), lambda b, g, i: (b * n_qt + i, g)),
        out_shape=jax.ShapeDtypeStruct((n_batch * t, n_groups * gw), BF16),
        scratch_shapes=[pltpu.VMEM((n_qt, TILE, TILE), F32)],
        compiler_params=_params("parallel", "parallel", "arbitrary"),
        name="nsa_prompt",
    )(rel_bias, proj, q_gain.reshape(1, HEAD_DIM), gl, kc, vc, ks, proj, kw, proj, dt)


def _nsa_sample_a_kernel(rb_ref, q_ref, qg_ref, gl_ref, kc_ref, vc_ref, wk_ref, wv_ref, kn_ref, vn_ref,
                         qn_ref, part_ref, sel_ref, *, ns, t_new, past_len, w_buf):
    g = pl.program_id(1)
    R = NSA_GROUP
    n_cols = kc_ref.shape[0]
    q = q_ref[...]
    qn = jnp.concatenate([_rms(q[:, r * HEAD_DIM:(r + 1) * HEAD_DIM], qg_ref[...]) for r in range(R)], axis=0)
    qb = qn.astype(BF16)
    gates = _sigmoid(gl_ref[...])
    q_pos = past_len + lax.broadcasted_iota(jnp.int32, (t_new, 1), 0)

    c_end = (lax.broadcasted_iota(jnp.int32, (1, n_cols), 1) + 1) * CMP_BLOCK - 1
    c_dist = q_pos - c_end
    c_bucket = _t5_bucket(c_dist)
    lc = _dot_nt(qb, kc_ref[...].astype(BF16)) * ATT_SCALE
    vcb = vc_ref[...].astype(BF16)
    imp = jnp.zeros((t_new, n_cols), F32)
    o_cmp = []
    for r in range(R):
        l_r = lc[r * t_new:(r + 1) * t_new] + _bias_from_bucket(c_bucket, rb_ref, g * R + r)
        p_r = _masked_softmax(l_r, c_dist >= 0)
        imp = imp + p_r
        o_cmp.append(_dot(p_r.astype(BF16), vcb))

    sel_ref[...] = _select_blocks(imp, q_pos, ns)

    pad = jnp.zeros((TILE - t_new, HEAD_DIM), F32)
    kwin = jnp.concatenate([wk_ref[...], kn_ref[...], pad], axis=0).astype(BF16)
    vwin = jnp.concatenate([wv_ref[...], vn_ref[...], pad], axis=0).astype(BF16)
    nw = w_buf + TILE
    col = lax.broadcasted_iota(jnp.int32, (1, nw), 1)
    w_pos = past_len - w_buf + col
    w_dist = q_pos - w_pos
    w_mask = (w_dist >= 0) & (w_dist < WINDOW) & (w_pos >= 0) & (col < w_buf + t_new)
    w_bucket = _t5_bucket(w_dist)
    lw = _dot_nt(qb, kwin) * ATT_SCALE
    for r in range(R):
        l_r = lw[r * t_new:(r + 1) * t_new] + _bias_from_bucket(w_bucket, rb_ref, g * R + r)
        o_win = _dot(_masked_softmax(l_r, w_mask).astype(BF16), vwin)
        c = (g * R + r) * N_BRANCH
        part_ref[:, r * HEAD_DIM:(r + 1) * HEAD_DIM] = _column(gates, c) * o_cmp[r] + _column(gates, c + 2) * o_win
        qn_ref[:, r * HEAD_DIM:(r + 1) * HEAD_DIM] = qn[r * t_new:(r + 1) * t_new]


def nsa_sample_a(proj, gl, kc, vc, win_k, win_v, kw_new, rel_bias, q_gain, *, n_batch, t_new, n_groups, past_len,
                 col_q, col_vw):
    ns_pad = kc.shape[2]
    ns = -(-(past_len + t_new) // CMP_BLOCK)
    w_buf = win_k.shape[1]
    gw = NSA_GROUP * HEAD_DIM
    cq, cvw = col_q // gw, col_vw // HEAD_DIM
    d_q = n_groups * gw
    return pl.pallas_call(
        functools.partial(_nsa_sample_a_kernel, ns=ns, t_new=t_new, past_len=past_len, w_buf=w_buf),
        grid=(n_batch, n_groups),
        in_specs=[pl.BlockSpec(memory_space=pltpu.SMEM),
                  pl.BlockSpec((t_new, gw), lambda b, g: (b, cq + g)),
                  pl.BlockSpec((1, HEAD_DIM), lambda b, g: (0, 0)),
                  pl.BlockSpec((t_new, LANES), lambda b, g: (b, 0)),
                  pl.BlockSpec((None, None, ns_pad, HEAD_DIM), lambda b, g: (b, g, 0, 0)),
                  pl.BlockSpec((None, None, ns_pad, HEAD_DIM), lambda b, g: (b, g, 0, 0)),
                  pl.BlockSpec((None, w_buf, HEAD_DIM), lambda b, g: (b, 0, g)),
                  pl.BlockSpec((None, w_buf, HEAD_DIM), lambda b, g: (b, 0, g)),
                  pl.BlockSpec((t_new, HEAD_DIM), lambda b, g: (b, g)),
                  pl.BlockSpec((t_new, HEAD_DIM), lambda b, g: (b, cvw + g))],
        out_specs=[pl.BlockSpec((t_new, gw), lambda b, g: (b, g)),
                   pl.BlockSpec((t_new, gw), lambda b, g: (b, g)),
                   pl.BlockSpec((None, None, t_new, ns_pad), lambda b, g: (b, g, 0, 0))],
        out_shape=[jax.ShapeDtypeStruct((n_batch * t_new, d_q), F32),
                   jax.ShapeDtypeStruct((n_batch * t_new, d_q), F32),
                   jax.ShapeDtypeStruct((n_batch, n_groups, t_new, ns_pad), F32)],
        compiler_params=_params("parallel", "parallel"),
        name="nsa_sample_a",
    )(rel_bias, proj, q_gain.reshape(1, HEAD_DIM), gl, kc, vc, win_k, win_v, kw_new, proj)


def _nsa_sample_b_kernel(*refs, n_pages, n_groups, t_new, past_len, pps):
    rb_ref, qn_ref, gl_ref, part_ref, mask_ref, nmask_ref = refs[1:7]
    k_refs, v_refs = refs[7:7 + pps], refs[7 + pps:7 + 2 * pps]
    kn_ref, vn_ref, o_ref, m_ref, l_ref, acc_ref, bias_ref = refs[7 + 2 * pps:]
    p = pl.program_id(1)
    n_steps = n_pages // pps
    R = NSA_GROUP
    rows_g = R * t_new
    rows = n_groups * rows_g
    row_head = lax.broadcasted_iota(jnp.int32, (rows, 1), 0) // t_new
    n_heads = n_groups * R

    def q_rows(g):
        qn = qn_ref[...]
        return jnp.concatenate([qn[:, (g * R + r) * HEAD_DIM:(g * R + r + 1) * HEAD_DIM] for r in range(R)],
                               axis=0).astype(BF16)

    def near_bias(base):
        qi = lax.broadcasted_iota(jnp.int32, (t_new, TILE), 0)
        kj = lax.broadcasted_iota(jnp.int32, (t_new, TILE), 1)
        bucket = _t5_bucket(qi - kj + base)
        return jnp.concatenate([_bias_from_bucket(bucket, rb_ref, h) for h in range(n_heads)], axis=0)

    @pl.when(p == 0)
    def _():
        m_ref[...] = jnp.full(m_ref.shape, NEG, F32)
        l_ref[...] = jnp.zeros(l_ref.shape, F32)
        acc_ref[...] = jnp.zeros(acc_ref.shape, F32)
        far = jnp.zeros((rows, 1), F32)
        for h in range(n_heads):
            far = jnp.where(row_head == h, rb_ref[FAR_BUCKET, h], far)
        bias_ref[...] = jnp.broadcast_to(far, bias_ref.shape)

    @pl.when(p == n_steps - 1)
    def _():
        bias_ref[:, (pps - 1) * PAGE_SIZE:] = near_bias(past_len - (n_pages - 1) * PAGE_SIZE)

    def update(k_of, v_of, mask_f, bias):
        s = jnp.concatenate([_dot_nt(q_rows(g), k_of(g)) for g in range(n_groups)], axis=0) * ATT_SCALE + bias
        mask = jnp.concatenate([jnp.concatenate([mask_f(g)] * R, axis=0) for g in range(n_groups)], axis=0) > 0.5
        s = jnp.where(mask, s, NEG)
        m_old = m_ref[...]
        m_new = jnp.maximum(m_old, jnp.max(s, axis=-1, keepdims=True))
        alpha = jnp.exp(m_old - m_new)
        pr = jnp.where(mask, jnp.exp(s - m_new), 0.0)
        l_ref[...] = alpha * l_ref[...] + jnp.sum(pr, axis=-1, keepdims=True)
        prb = pr.astype(BF16)
        pv = jnp.concatenate([_dot(prb[g * rows_g:(g + 1) * rows_g], v_of(g)) for g in range(n_groups)], axis=0)
        acc_ref[...] = alpha * acc_ref[...] + pv
        m_ref[...] = m_new

    def pages_of(page_refs):
        return lambda g: jnp.concatenate([_page_group(r, g, n_groups, True) for r in page_refs], axis=0).astype(BF16)

    update(pages_of(k_refs), pages_of(v_refs), lambda g: mask_ref[g], bias_ref[...])

    @pl.when(p == n_steps - 1)
    def _():
        qi = lax.broadcasted_iota(jnp.int32, (t_new, TILE), 0)
        kj = lax.broadcasted_iota(jnp.int32, (t_new, TILE), 1)
        causal = (qi - kj >= 0) & (kj < t_new)
        pad = jnp.zeros((TILE - t_new, n_groups * HEAD_DIM), F32)
        k_new = jnp.concatenate([kn_ref[...], pad], axis=0).astype(BF16)
        v_new = jnp.concatenate([vn_ref[...], pad], axis=0).astype(BF16)
        update(lambda g: k_new[:, g * HEAD_DIM:(g + 1) * HEAD_DIM], lambda g: v_new[:, g * HEAD_DIM:(g + 1) * HEAD_DIM],
               lambda g: jnp.where(causal, nmask_ref[g], 0.0), near_bias(0))
        o_sel = acc_ref[...] / jnp.maximum(l_ref[...], 1e-30)
        gates = _sigmoid(gl_ref[...])
        part = part_ref[...]
        for h in range(n_heads):
            gs = gates[:, h * N_BRANCH + 1:h * N_BRANCH + 2]
            o = part[:, h * HEAD_DIM:(h + 1) * HEAD_DIM] + gs * o_sel[h * t_new:(h + 1) * t_new]
            o_ref[:, h * HEAD_DIM:(h + 1) * HEAD_DIM] = o


def nsa_sample_b(table, rel_bias, qn, gl, part, mask, nmask, cache_k, cache_v, ks_new, proj, *, n_batch, t_new,
                 n_groups, n_pages, past_len, col_vs):
    width = n_groups * HEAD_DIM
    d_q = qn.shape[1]
    rows = n_groups * NSA_GROUP * t_new
    cvs = col_vs // width
    pps = _pick(n_pages, (PAGES_PER_STEP, 2, 1))

    def page_spec(i):
        return pl.BlockSpec((PAGE_SIZE * n_groups, HEAD_DIM), lambda b, s, t: (t[b * n_pages + s * pps + i], 0))

    grid_spec = pltpu.PrefetchScalarGridSpec(
        num_scalar_prefetch=1,
        grid=(n_batch, n_pages // pps),
        in_specs=[pl.BlockSpec(memory_space=pltpu.SMEM),
                  pl.BlockSpec((t_new, d_q), lambda b, s, t: (b, 0)),
                  pl.BlockSpec((t_new, LANES), lambda b, s, t: (b, 0)),
                  pl.BlockSpec((t_new, d_q), lambda b, s, t: (b, 0)),
                  pl.BlockSpec((None, n_groups, t_new, pps * PAGE_SIZE), lambda b, s, t: (b, 0, 0, s)),
                  pl.BlockSpec((None, n_groups, t_new, LANES), lambda b, s, t: (b, 0, 0, 0))]
        + [page_spec(i) for i in range(pps)] * 2
        + [pl.BlockSpec((t_new, width), lambda b, s, t: (b, 0)),
           pl.BlockSpec((t_new, width), lambda b, s, t: (b, cvs))],
        out_specs=pl.BlockSpec((t_new, d_q), lambda b, s, t: (b, 0)),
        scratch_shapes=[pltpu.VMEM((rows, 1), F32), pltpu.VMEM((rows, 1), F32),
                        pltpu.VMEM((rows, HEAD_DIM), F32), pltpu.VMEM((rows, pps * PAGE_SIZE), F32)],
    )
    return pl.pallas_call(
        functools.partial(_nsa_sample_b_kernel, n_pages=n_pages, n_groups=n_groups, t_new=t_new, past_len=past_len,
                          pps=pps),
        grid_spec=grid_spec,
        out_shape=jax.ShapeDtypeStruct((n_batch * t_new, d_q), F32),
        compiler_params=_params("parallel", "arbitrary"),
        name="nsa_sample_b",
    )(table, rel_bias, qn, gl, part, mask, nmask, *([cache_k] * pps), *([cache_v] * pps), ks_new, proj)


def _sb_tile(z, valid, carry, upper):
    rows = z.shape[0]
    w = upper.shape[0]
    n = z.shape[1] // w
    ls = _log_sigmoid(z)
    lk = ls - z
    if valid is not None:
        lk = jnp.where(valid, lk, 0.0)
    chunks = [None] * n
    for c in reversed(range(n)):
        lkc = lk[:, c * w:(c + 1) * w]
        hi = lkc.astype(BF16)
        lo = (lkc - hi.astype(F32)).astype(BF16)
        both = _dot(jnp.concatenate([hi, lo], axis=0), upper)
        a = jnp.exp(ls[:, c * w:(c + 1) * w] + (both[:rows] + both[rows:]) + carry)
        if valid is not None:
            a = jnp.where(valid[:, c * w:(c + 1) * w], a, 0.0)
        chunks[c] = a
        carry = carry + jnp.sum(lkc, axis=-1, keepdims=True)
    return (chunks[0] if n == 1 else jnp.concatenate(chunks, axis=1)), carry


def _strict_upper(n):
    j = lax.broadcasted_iota(jnp.int32, (n, n), 0)
    s = lax.broadcasted_iota(jnp.int32, (n, n), 1)
    return jnp.where(j > s, 1.0, 0.0).astype(BF16)


SB_TQ = 512
SB_TK = 256


def _sb_prompt_kernel(q_ref, k_ref, v_ref, o_ref, *, tq, tk):
    qi = pl.program_id(2)
    qb = q_ref[...].astype(BF16)
    upper = _strict_upper(tk)
    n_diag = tq // tk

    def tile(start, q_rows, valid, carry, acc):
        k = k_ref[pl.ds(start, tk), :].astype(BF16)
        v = v_ref[pl.ds(start, tk), :].astype(BF16)
        z = _dot_nt(q_rows, k) * ATT_SCALE
        a, carry = _sb_tile(z, valid, carry, upper)
        return carry, acc + _dot(a.astype(BF16), v)

    carry = jnp.zeros((tq, 1), F32)
    acc = jnp.zeros((tq, HEAD_DIM), F32)
    for d in reversed(range(n_diag)):
        r0 = d * tk
        ii = lax.broadcasted_iota(jnp.int32, (tq - r0, tk), 0)
        jj = lax.broadcasted_iota(jnp.int32, (tq - r0, tk), 1)
        start = pl.multiple_of(qi * tq + r0, tk)
        c_new, a_new = tile(start, qb[r0:], jj < ii, carry[r0:], acc[r0:])
        carry = c_new if r0 == 0 else jnp.concatenate([carry[:r0], c_new], axis=0)
        acc = a_new if r0 == 0 else jnp.concatenate([acc[:r0], a_new], axis=0)

    def body(n, c):
        start = pl.multiple_of((qi * n_diag - 1 - n) * tk, tk)
        return tile(start, qb, None, *c)

    _, acc = lax.fori_loop(0, qi * n_diag, body, (carry, acc))
    o_ref[...] = acc.astype(o_ref.dtype)


def sb_prompt(q, k, v, *, n_batch, t, n_heads):
    tq = _pick(t, (SB_TQ, SB_TK))
    tk = SB_TK
    n_qt = t // tq
    return pl.pallas_call(
        functools.partial(_sb_prompt_kernel, tq=tq, tk=tk),
        grid=(n_batch, n_heads, n_qt),
        in_specs=[pl.BlockSpec((tq, HEAD_DIM), lambda b, h, i: (b * n_qt + i, h)),
                  pl.BlockSpec((t, HEAD_DIM), lambda b, h, i: (b, h)),
                  pl.BlockSpec((t, HEAD_DIM), lambda b, h, i: (b, h))],
        out_specs=pl.BlockSpec((tq, HEAD_DIM), lambda b, h, i: (b * n_qt + i, h)),
        out_shape=jax.ShapeDtypeStruct((n_batch * t, n_heads * HEAD_DIM), BF16),
        compiler_params=_params("parallel", "parallel", "arbitrary"),
        name="sb_prompt",
    )(q, k, v)


SUBLANES = 8


def _sb_sample_kernel(tbl_ref, q_ref, kn_ref, vn_ref, k_ref, v_ref, o_ref, qs_ref, c_ref, acc_ref, *,
                      n_pages, n_heads, t_new):
    del tbl_ref
    p = pl.program_id(1)
    upper = _strict_upper(SB_TK)
    hb_rows = SUBLANES * t_new
    n_lanes = PAGE_SIZE * SUBLANES

    @pl.when(p == 0)
    def _():
        q = q_ref[...]
        qs = jnp.concatenate([q[:, h * HEAD_DIM:(h + 1) * HEAD_DIM] for h in range(n_heads)], axis=0)
        qs_ref[...] = qs
        rows = n_heads * t_new
        qi = lax.broadcasted_iota(jnp.int32, (rows, TILE), 0) % t_new
        kj = lax.broadcasted_iota(jnp.int32, (rows, TILE), 1)
        pad = jnp.zeros((TILE - t_new, n_heads * HEAD_DIM), F32)
        kb = jnp.concatenate([kn_ref[...], pad], axis=0).astype(BF16)
        vb = jnp.concatenate([vn_ref[...], pad], axis=0).astype(BF16)
        qsb = qs.astype(BF16)
        z = jnp.concatenate([_dot_nt(qsb[h * t_new:(h + 1) * t_new], kb[:, h * HEAD_DIM:(h + 1) * HEAD_DIM])
                             for h in range(n_heads)], axis=0) * ATT_SCALE
        a, carry = _sb_tile(z, kj < qi, jnp.zeros((rows, 1), F32), upper[:TILE, :TILE])
        ab = a.astype(BF16)
        acc_ref[...] = jnp.concatenate([_dot(ab[h * t_new:(h + 1) * t_new], vb[:, h * HEAD_DIM:(h + 1) * HEAD_DIM])
                                        for h in range(n_heads)], axis=0)
        c_ref[...] = carry

    lane = lax.broadcasted_iota(jnp.int32, (hb_rows, n_lanes), 1)
    row = lax.broadcasted_iota(jnp.int32, (hb_rows, n_lanes), 0)
    own = lane % SUBLANES == row // t_new
    for hb in range(n_heads // SUBLANES):
        rs = slice(hb * hb_rows, (hb + 1) * hb_rows)
        kb = k_ref[:, hb * SUBLANES:(hb + 1) * SUBLANES, :].reshape(n_lanes, HEAD_DIM).astype(BF16)
        vb = v_ref[:, hb * SUBLANES:(hb + 1) * SUBLANES, :].reshape(n_lanes, HEAD_DIM).astype(BF16)
        z = _dot_nt(qs_ref[rs, :].astype(BF16), kb) * ATT_SCALE
        a, carry = _sb_tile(z, own, c_ref[rs, :], upper)
        acc_ref[rs, :] += _dot(a.astype(BF16), vb)
        c_ref[rs, :] = carry

    @pl.when(p == n_pages - 1)
    def _():
        acc = acc_ref[...]
        for h in range(n_heads):
            o_ref[:, h * HEAD_DIM:(h + 1) * HEAD_DIM] = acc[h * t_new:(h + 1) * t_new]


def sb_sample(table, q, k, v, cache_k, cache_v, layer, *, n_batch, t_new, n_heads, n_pages):
    d = n_heads * HEAD_DIM
    rows = n_heads * t_new
    assert n_heads % SUBLANES == 0 and (PAGE_SIZE * SUBLANES) % SB_TK == 0
    page = lambda b, p, t: (layer, t[b * n_pages + n_pages - 1 - p], 0, 0, 0)
    grid_spec = pltpu.PrefetchScalarGridSpec(
        num_scalar_prefetch=1,
        grid=(n_batch, n_pages),
        in_specs=[pl.BlockSpec((t_new, d), lambda b, p, t: (b, 0)),
                  pl.BlockSpec((t_new, d), lambda b, p, t: (b, 0)),
                  pl.BlockSpec((t_new, d), lambda b, p, t: (b, 0)),
                  pl.BlockSpec((None, None, PAGE_SIZE, n_heads, HEAD_DIM), page),
                  pl.BlockSpec((None, None, PAGE_SIZE, n_heads, HEAD_DIM), page)],
        out_specs=pl.BlockSpec((t_new, d), lambda b, p, t: (b, 0)),
        scratch_shapes=[pltpu.VMEM((rows, HEAD_DIM), F32), pltpu.VMEM((rows, 1), F32),
                        pltpu.VMEM((rows, HEAD_DIM), F32)],
    )
    return pl.pallas_call(
        functools.partial(_sb_sample_kernel, n_pages=n_pages, n_heads=n_heads, t_new=t_new),
        grid_spec=grid_spec,
        out_shape=jax.ShapeDtypeStruct((n_batch * t_new, d), F32),
        compiler_params=_params("parallel", "arbitrary"),
        name="sb_sample",
    )(table, q, k, v, cache_k, cache_v)


def _pick(n, candidates):
    for c in candidates:
        if n % c == 0:
            return c
    return n


def _mm(xs, w, layer, res=None, nk=1, col_off=0, n_cols=None, name="matmul"):
    m = xs[0].shape[0]
    n = w.shape[2] - col_off if n_cols is None else n_cols
    tm = _pick(m, (1024, 512, 256, 128))
    tn = _pick(n, (512, 256, 128))
    return matmul(xs, w, res, tm=tm, tn=tn, nk=nk, layer=layer, col_off=col_off, n_cols=n_cols, name=name)


def kernel(x_prompt, x_sample, state_conv_a, cache_cmp_k, cache_cmp_v, cache_sel_k, cache_sel_v, state_win_k,
           state_win_v, cache_sb_k, cache_sb_v, state_ffn_conv, page_table, norm_mix, norm_ffn, w_in_even, conv_a_w,
           q_norm, k_norm, w_cmp_k, w_cmp_v, pe_cmp, rel_bias, w_out_even, w_qkv_odd, w_out_odd, w_ffn_gate,
           w_ffn_up, ffn_conv_w, w_ffn_down):
    bp, seq, d_model = x_prompt.shape
    bs, t_new, _ = x_sample.shape
    depth = norm_mix.shape[0]
    n_pages = page_table.shape[1]
    past_len = n_pages * PAGE_SIZE
    d_conv = conv_a_w.shape[2]
    n_heads_nsa = rel_bias.shape[1]
    n_groups = n_heads_nsa // NSA_GROUP
    d_q = n_heads_nsa * HEAD_DIM
    d_kv = n_groups * HEAD_DIM
    sb_heads = w_out_odd.shape[1] // HEAD_DIM
    d_sb = sb_heads * HEAD_DIM
    d_ff = w_ffn_gate.shape[2]
    n_gate_cols = N_BRANCH * n_heads_nsa
    col_q = 3 * d_conv
    col_kv = col_q + d_q
    col_gl = col_kv + 2 * N_BRANCH * d_kv
    assert n_gate_cols <= LANES and seq % TILE == 0 and seq >= WINDOW and t_new <= CMP_BLOCK

    xp = x_prompt.reshape(bp * seq, d_model)
    xs = x_sample.reshape(bs * t_new, d_model)
    table = page_table.reshape(-1).astype(jnp.int32)
    pages_p = seq // PAGE_SIZE
    table_p = jnp.arange(bp * pages_p, dtype=jnp.int32)
    zeros_conv_p = jnp.zeros((bp, 2, d_conv), F32)
    zeros_ffn_p = jnp.zeros((bp, 2, d_ff), F32)
    dt = bias_tiles(rel_bias)

    w_in_b = w_in_even.astype(BF16)
    w_gl_b = jnp.pad(w_in_even[:, :, col_gl:], ((0, 0), (0, 0), (0, LANES - n_gate_cols))).astype(BF16)
    w_out_even_b, w_qkv_b, w_out_odd_b = (w.astype(BF16) for w in (w_out_even, w_qkv_odd, w_out_odd))
    w_gate_b, w_up_b, w_down_b = (w.astype(BF16) for w in (w_ffn_gate, w_ffn_up, w_ffn_down))

    outs = {k: [] for k in ("conv_p", "conv_s", "cmp_kp", "cmp_vp", "cmp_ks", "cmp_vs", "sel_kp", "sel_vp", "sel_ks",
                            "sel_vs", "win_kp", "win_vp", "win_ks", "win_vs", "sb_kp", "sb_vp", "sb_ks", "sb_vs",
                            "ffn_p", "ffn_s")}

    def kv4(a, b, t):
        return a.reshape(b, t, n_groups, HEAD_DIM)

    for layer in range(depth):
        hp = rmsnorm_bf16(xp, norm_mix[layer])
        hs = rmsnorm_bf16(xs, norm_mix[layer])
        if layer % 2 == 0:
            e = layer // 2
            wk, wv = w_cmp_k[e].astype(BF16), w_cmp_v[e].astype(BF16)
            cols = [col_kv + i * d_kv for i in range(2 * N_BRANCH)]

            proj = _mm([hp], w_in_b, e, n_cols=col_gl, name="in_proj_p")
            gl = _mm([hp], w_gl_b, e, name="gate_proj_p")
            y_a, hist_p = conv_gate(proj, zeros_conv_p, conv_a_w[e], n_batch=bp, t=seq, d_conv=d_conv,
                                    out_dtype=BF16)
            ks_n = headnorm(proj, cols[2], n_groups, k_norm[e, 1])
            kw_n = headnorm(proj, cols[4], n_groups, k_norm[e, 2])
            kc, vc = compress(proj, proj, cols[0], cols[1], table_p, pe_cmp[e], wk, wv, k_norm[e, 0],
                              n_batch=bp, n_pages=pages_p, chunk=pages_p, n_groups=n_groups, by_rows=False)
            o = nsa_prompt(proj, gl, kc, vc, ks_n, kw_n, dt, rel_bias, q_norm[e], n_batch=bp, t=seq,
                           n_groups=n_groups, col_q=col_q, col_vs=cols[3], col_vw=cols[5])
            xp = _mm([y_a, o], w_out_even_b, e, xp, name="out_proj_p")
            outs["conv_p"].append(hist_p)
            outs["cmp_kp"].append(kv4(proj[:, cols[0]:cols[0] + d_kv], bp, seq))
            outs["cmp_vp"].append(kv4(proj[:, cols[1]:cols[1] + d_kv], bp, seq))
            outs["sel_kp"].append(kv4(ks_n, bp, seq))
            outs["sel_vp"].append(kv4(proj[:, cols[3]:cols[3] + d_kv], bp, seq))
            n_keep = min(WINDOW, seq)
            outs["win_kp"].append(kv4(kw_n, bp, seq)[:, seq - n_keep:])
            outs["win_vp"].append(kv4(proj[:, cols[5]:cols[5] + d_kv], bp, seq)[:, seq - n_keep:])

            proj = _mm([hs], w_in_b, e, n_cols=col_gl, name="in_proj_s")
            gl = _mm([hs], w_gl_b, e, name="gate_proj_s")
            y_a, hist_s = conv_gate(proj, state_conv_a[e], conv_a_w[e], n_batch=bs, t=t_new, d_conv=d_conv,
                                    out_dtype=F32)
            ks_n = headnorm(proj, cols[2], n_groups, k_norm[e, 1])
            kw_n = headnorm(proj, cols[4], n_groups, k_norm[e, 2])
            pool = lambda c: c[e].reshape(-1, HEAD_DIM)
            chunk = _pick(n_pages, (32, 16, 8, 4, 2, 1))
            kc, vc = compress(pool(cache_cmp_k), pool(cache_cmp_v), 0, 0, table, pe_cmp[e], wk, wv, k_norm[e, 0],
                              n_batch=bs, n_pages=n_pages, chunk=chunk, n_groups=n_groups, by_rows=True)
            ns = -(-(past_len + t_new) // CMP_BLOCK)
            pad_rows = ((0, 0), (0, 0), (0, _round_up(ns, LANES) - kc.shape[2]), (0, 0))
            kc, vc = jnp.pad(kc, pad_rows), jnp.pad(vc, pad_rows)
            w_buf = state_win_k.shape[2]
            win_k = state_win_k[e].reshape(bs, w_buf, d_kv)
            win_v = state_win_v[e].reshape(bs, w_buf, d_kv)
            qn, part, sel = nsa_sample_a(proj, gl, kc, vc, win_k, win_v, kw_n, rel_bias, q_norm[e], n_batch=bs,
                                         t_new=t_new, n_groups=n_groups, past_len=past_len, col_q=col_q,
                                         col_vw=cols[5])
            n_past_blocks = past_len // CMP_BLOCK
            mask = jnp.repeat(sel[..., :n_past_blocks], CMP_BLOCK, axis=-1)
            nmask = jnp.broadcast_to(sel[..., n_past_blocks:n_past_blocks + 1], sel.shape[:-1] + (LANES,))
            o = nsa_sample_b(table, rel_bias, qn, gl, part, mask, nmask, pool(cache_sel_k), pool(cache_sel_v), ks_n,
                             proj, n_batch=bs, t_new=t_new, n_groups=n_groups, n_pages=n_pages, past_len=past_len,
                             col_vs=cols[3])
            xs = _mm([y_a.astype(BF16), o.astype(BF16)], w_out_even_b, e, xs, name="out_proj_s")
            outs["conv_s"].append(hist_s)
            outs["cmp_ks"].append(kv4(proj[:, cols[0]:cols[0] + d_kv], bs, t_new))
            outs["cmp_vs"].append(kv4(proj[:, cols[1]:cols[1] + d_kv], bs, t_new))
            outs["sel_ks"].append(kv4(ks_n, bs, t_new))
            outs["sel_vs"].append(kv4(proj[:, cols[3]:cols[3] + d_kv], bs, t_new))
            kw_all = jnp.concatenate([state_win_k[e], kv4(kw_n, bs, t_new)], axis=1)
            vw_all = jnp.concatenate([state_win_v[e], kv4(proj[:, cols[5]:cols[5] + d_kv], bs, t_new)], axis=1)
            outs["win_ks"].append(kw_all[:, kw_all.shape[1] - w_buf:])
            outs["win_vs"].append(vw_all[:, vw_all.shape[1] - w_buf:])
        else:
            o_ix = layer // 2
            q, k, v = (_mm([hp], w_qkv_b, o_ix, col_off=i * d_sb, n_cols=d_sb, name=n)
                       for i, n in enumerate(("q_p", "k_p", "v_p")))
            o = sb_prompt(q, k, v, n_batch=bp, t=seq, n_heads=sb_heads)
            xp = _mm([o], w_out_odd_b, o_ix, xp, name="sb_out_p")
            outs["sb_kp"].append(k.reshape(bp, seq, sb_heads, HEAD_DIM))
            outs["sb_vp"].append(v.reshape(bp, seq, sb_heads, HEAD_DIM))

            q, k, v = (_mm([hs], w_qkv_b, o_ix, col_off=i * d_sb, n_cols=d_sb, name=n)
                       for i, n in enumerate(("q_s", "k_s", "v_s")))
            o = sb_sample(table, q, k, v, cache_sb_k, cache_sb_v, o_ix, n_batch=bs, t_new=t_new, n_heads=sb_heads,
                          n_pages=n_pages)
            xs = _mm([o.astype(BF16)], w_out_odd_b, o_ix, xs, name="sb_out_s")
            outs["sb_ks"].append(k.reshape(bs, t_new, sb_heads, HEAD_DIM))
            outs["sb_vs"].append(v.reshape(bs, t_new, sb_heads, HEAD_DIM))

        nk_down = 2 if d_ff % (2 * LANES) == 0 else 1
        tn_ff = _pick(d_ff, (256, 128))
        gate, up = matmul_pair(rmsnorm_bf16(xp, norm_ffn[layer]), w_gate_b, w_up_b, layer,
                               tm=_pick(bp * seq, (1024, 512, 256, 128)), tn=tn_ff, name="ffn_gu_p")
        act, hf_p = ffn_act(gate, up, zeros_ffn_p, ffn_conv_w[layer], n_batch=bp, t=seq, d_ff=d_ff, out_dtype=BF16)
        xp = _mm([act], w_down_b, layer, xp, nk=nk_down, name="ffn_down_p")
        gate, up = matmul_pair(rmsnorm_bf16(xs, norm_ffn[layer]), w_gate_b, w_up_b, layer, tm=bs * t_new, tn=tn_ff,
                               name="ffn_gu_s")
        act, hf_s = ffn_act(gate, up, state_ffn_conv[layer], ffn_conv_w[layer], n_batch=bs, t=t_new, d_ff=d_ff,
                            out_dtype=F32)
        xs = _mm([act.astype(BF16)], w_down_b, layer, xs, nk=nk_down, name="ffn_down_s")
        outs["ffn_p"].append(hf_p)
        outs["ffn_s"].append(hf_s)

    st = lambda key: jnp.stack(outs[key], axis=0)
    return (xp.reshape(bp, seq, d_model), xs.reshape(bs, t_new, d_model), st("conv_p"), st("conv_s"),
            st("cmp_kp"), st("cmp_vp"), st("cmp_ks"), st("cmp_vs"),
            st("sel_kp"), st("sel_vp"), st("sel_ks"), st("sel_vs"),
            st("win_kp"), st("win_vp"), st("win_ks"), st("win_vs"),
            st("sb_kp"), st("sb_vp"), st("sb_ks"), st("sb_vs"),
            st("ffn_p"), st("ffn_s"))
```

```python
import functools
import math

import jax
import jax.numpy as jnp
from jax import lax
from jax.experimental import pallas as pl
from jax.experimental.pallas import tpu as pltpu

F32 = jnp.float32
BF16 = jnp.bfloat16

HEAD_DIM = 128
LANES = 128
PAGE_SIZE = 128
CMP_BLOCK = 64
TOP_N = 16
WINDOW = 512
NSA_GROUP = 4
N_BRANCH = 3
N_BUCKETS = 32
MAX_DISTANCE = 128
MAX_EXACT = N_BUCKETS // 2
FAR_BUCKET = N_BUCKETS - 1
FORCED_SCORE = float(NSA_GROUP + 1)
RMS_EPS = 1e-6
NEG = -1e30
ATT_SCALE = HEAD_DIM ** -0.5
TILE = 128
CONV_TILE_ELEMS = 512 * 1024
VMEM_LIMIT = 56 * 1024 * 1024
PROJECT_VMEM_BUDGET = 48 * 1024 * 1024


def _params(*sem):
    return pltpu.CompilerParams(dimension_semantics=sem, vmem_limit_bytes=VMEM_LIMIT)


def _dot(a, b):
    return jnp.dot(a, b, preferred_element_type=F32)


def _dot_nt(a, b):
    return lax.dot_general(a, b, (((1,), (1,)), ((), ())), preferred_element_type=F32)


def _rms(x, gain):
    return x * lax.rsqrt(jnp.mean(x * x, axis=-1, keepdims=True) + RMS_EPS) * gain


def _sigmoid(x):
    return 1.0 / (1.0 + jnp.exp(-x))


def _t5_bucket(dist):
    n = jnp.maximum(dist, 0)
    nf = jnp.maximum(n, 1).astype(F32)
    large = MAX_EXACT + (jnp.log(nf / MAX_EXACT) / math.log(MAX_DISTANCE / MAX_EXACT)
                         * (N_BUCKETS - MAX_EXACT)).astype(jnp.int32)
    large = jnp.minimum(large, N_BUCKETS - 1)
    return jnp.where(n < MAX_EXACT, n, large)


def _bias_from_bucket(bucket, rb_ref, head):
    out = jnp.zeros(bucket.shape, F32)
    for k in range(N_BUCKETS):
        out = jnp.where(bucket == k, rb_ref[k, head], out)
    return out


def _masked_softmax(l, mask):
    l = jnp.where(mask, l, NEG)
    m = jnp.max(l, axis=-1, keepdims=True)
    e = jnp.where(mask, jnp.exp(l - m), 0.0)
    return e / jnp.maximum(jnp.sum(e, axis=-1, keepdims=True), 1e-30)


def _log_sigmoid(z):
    return jnp.minimum(z, 0.0) - jnp.log(1.0 + jnp.exp(-jnp.abs(z)))


def _column(x, c):
    lane = lax.broadcasted_iota(jnp.int32, x.shape, 1)
    return jnp.sum(jnp.where(lane == c, x, 0.0), axis=1, keepdims=True)


def _round_up(n, m):
    return -(-n // m) * m


def _rmsnorm_kernel(x_ref, g_ref, o_ref):
    o_ref[...] = _rms(x_ref[...], g_ref[...]).astype(o_ref.dtype)


def rmsnorm_bf16(x, gain):
    m, d = x.shape
    tr = min(m, 256)
    return pl.pallas_call(
        _rmsnorm_kernel,
        grid=(m // tr,),
        in_specs=[pl.BlockSpec((tr, d), lambda i: (i, 0)),
                  pl.BlockSpec((1, d), lambda i: (0, 0))],
        out_specs=pl.BlockSpec((tr, d), lambda i: (i, 0)),
        out_shape=jax.ShapeDtypeStruct((m, d), BF16),
        compiler_params=_params("parallel"),
        name="rmsnorm",
    )(x, gain.reshape(1, d))


def _matmul_kernel(*refs, n_x, k_sizes, has_res, nk):
    x_refs = refs[:n_x]
    w_ref = refs[n_x]
    r_ref = refs[n_x + 1] if has_res else None
    o_ref = refs[n_x + 1 + int(has_res)]
    acc_ref = refs[-1] if nk > 1 else None

    acc = None
    k0 = 0
    for x_ref, ks in zip(x_refs, k_sizes):
        part = _dot(x_ref[...], w_ref[k0:k0 + ks, :])
        acc = part if acc is None else acc + part
        k0 += ks

    def finish(total):
        if has_res:
            total = total + r_ref[...]
        o_ref[...] = total.astype(o_ref.dtype)

    if nk == 1:
        finish(acc)
    else:
        k = pl.program_id(2)

        @pl.when(k == 0)
        def _():
            acc_ref[...] = acc

        @pl.when(k > 0)
        def _():
            acc_ref[...] += acc

        @pl.when(k == nk - 1)
        def _():
            finish(acc_ref[...])


def matmul(xs, w, res=None, *, tm, tn, nk=1, layer=0, col_off=0, n_cols=None, out_dtype=F32, name="matmul"):
    m = xs[0].shape[0]
    k_sizes = tuple(x.shape[1] for x in xs)
    _, k_total, n_w = w.shape
    n = n_w - col_off if n_cols is None else n_cols
    assert sum(k_sizes) == k_total and m % tm == 0 and n % tn == 0 and col_off % tn == 0
    assert nk == 1 or len(xs) == 1
    tk = k_total // nk
    jb = col_off // tn
    if nk == 1:
        x_specs = [pl.BlockSpec((tm, ks), lambda i, j, k: (i, 0)) for ks in k_sizes]
        k_blk = k_sizes
    else:
        x_specs = [pl.BlockSpec((tm, tk), lambda i, j, k: (i, k))]
        k_blk = (tk,)
    in_specs = x_specs + [pl.BlockSpec((None, tk, tn), lambda i, j, k: (layer, k, jb + j))]
    args = list(xs) + [w]
    if res is not None:
        in_specs.append(pl.BlockSpec((tm, tn), lambda i, j, k: (i, j)))
        args.append(res)
    scratch = [pltpu.VMEM((tm, tn), F32)] if nk > 1 else []
    return pl.pallas_call(
        functools.partial(_matmul_kernel, n_x=len(xs), k_sizes=k_blk, has_res=res is not None, nk=nk),
        grid=(m // tm, n // tn, nk),
        in_specs=in_specs,
        out_specs=pl.BlockSpec((tm, tn), lambda i, j, k: (i, j)),
        out_shape=jax.ShapeDtypeStruct((m, n), out_dtype),
        scratch_shapes=scratch,
        compiler_params=_params("parallel", "parallel", "arbitrary"),
        name=name,
    )(*args)


def _project_kernel(*refs, n_x, k_sizes, n_w, has_res):
    refs = list(refs)
    x_refs, s_refs = refs[:n_x], refs[n_x:2 * n_x]
    w_refs = refs[2 * n_x:2 * n_x + n_w]
    pos = 2 * n_x + n_w
    r_ref, rs_ref = (refs[pos], refs[pos + 1]) if has_res else (None, None)
    pos += 2 * int(has_res)
    o_refs, os_refs = refs[pos:pos + n_w], refs[pos + n_w:pos + 2 * n_w]
    for w_ref, o_ref, os_ref in zip(w_refs, o_refs, os_refs):
        acc = acc_s = None
        k0 = 0
        for x_ref, s_ref, ks in zip(x_refs, s_refs, k_sizes):
            w = w_ref[k0:k0 + ks, :].astype(BF16)
            part, part_s = _dot(x_ref[...], w), _dot(s_ref[...], w)
            acc = part if acc is None else acc + part
            acc_s = part_s if acc_s is None else acc_s + part_s
            k0 += ks
        if has_res:
            acc, acc_s = acc + r_ref[...], acc_s + rs_ref[...]
        o_ref[...] = acc
        os_ref[0] = acc_s


def project(xs, side_xs, ws, layer, res=None, side_res=None, *, tm, tn, col_off=0, n_cols=None, name):
    m, ms = xs[0].shape[0], side_xs[0].shape[0]
    k_sizes = tuple(x.shape[1] for x in xs)
    k_total, n_full = ws[0].shape[1:]
    n = n_full - col_off if n_cols is None else n_cols
    assert sum(k_sizes) == k_total and m % tm == 0 and n % tn == 0 and col_off % tn == 0
    jb = col_off // tn
    in_specs = [pl.BlockSpec((tm, ks), lambda i, j: (i, 0), pipeline_mode=pl.Buffered(1)) for ks in k_sizes]
    in_specs += [pl.BlockSpec((ms, ks), lambda i, j: (0, 0)) for ks in k_sizes]
    in_specs += [pl.BlockSpec((None, k_total, tn), lambda i, j: (layer, 0, jb + j)) for _ in ws]
    args = list(xs) + list(side_xs) + list(ws)
    if res is not None:
        in_specs += [pl.BlockSpec((tm, tn), lambda i, j: (i, j)), pl.BlockSpec((ms, tn), lambda i, j: (0, j))]
        args += [res, side_res]
    out_specs = [pl.BlockSpec((tm, tn), lambda i, j: (i, j)) for _ in ws]
    out_specs += [pl.BlockSpec((1, ms, tn), lambda i, j: (i, 0, j)) for _ in ws]
    out_shape = [jax.ShapeDtypeStruct((m, n), F32) for _ in ws]
    out_shape += [jax.ShapeDtypeStruct((m // tm, ms, n), F32) for _ in ws]
    outs = pl.pallas_call(
        functools.partial(_project_kernel, n_x=len(xs), k_sizes=k_sizes, n_w=len(ws), has_res=res is not None),
        grid=(m // tm, n // tn),
        in_specs=in_specs,
        out_specs=out_specs,
        out_shape=out_shape,
        compiler_params=_params("parallel", "parallel"),
        name=name,
    )(*args)
    return list(outs[:len(ws)]) + [o[0] for o in outs[len(ws):]]


def _conv3(u, hist, w):
    t = u.shape[0]
    row = lax.broadcasted_iota(jnp.int32, u.shape, 0)
    h0, h1 = hist[0:1], hist[1:2]
    u1 = jnp.where(row == 0, h1, pltpu.roll(u, 1, 0))
    u2 = jnp.where(row == 0, h0, jnp.where(row == 1, h1, pltpu.roll(u, 2, 0)))
    y = w[2:3] * u
    y = y + w[0:1] * u2
    y = y + w[1:2] * u1
    return y, u[t - 2:t]


def _conv_gate_kernel(xa_ref, ba_ref, ca_ref, hist_ref, w_ref, y_ref, nh_ref):
    u = ca_ref[...] * xa_ref[...]
    conv, new_hist = _conv3(u, hist_ref[0], w_ref[...])
    y_ref[...] = (ba_ref[...] * conv).astype(y_ref.dtype)
    nh_ref[0] = new_hist


def _conv_cols(t, width):
    cols = min(width, max(LANES, CONV_TILE_ELEMS // t // LANES * LANES))
    while width % cols:
        cols -= LANES
    return cols


def conv_gate(proj, hist, w, *, n_batch, t, d_conv, out_dtype):
    tc = _conv_cols(t, d_conv)
    nb = d_conv // tc
    return pl.pallas_call(
        _conv_gate_kernel,
        grid=(n_batch, nb),
        in_specs=[pl.BlockSpec((t, tc), lambda b, j: (b, j)),
                  pl.BlockSpec((t, tc), lambda b, j: (b, nb + j)),
                  pl.BlockSpec((t, tc), lambda b, j: (b, 2 * nb + j)),
                  pl.BlockSpec((1, 2, tc), lambda b, j: (b, 0, j)),
                  pl.BlockSpec((3, tc), lambda b, j: (0, j))],
        out_specs=[pl.BlockSpec((t, tc), lambda b, j: (b, j)),
                   pl.BlockSpec((1, 2, tc), lambda b, j: (b, 0, j))],
        out_shape=[jax.ShapeDtypeStruct((n_batch * t, d_conv), out_dtype),
                   jax.ShapeDtypeStruct((n_batch, 2, d_conv), F32)],
        compiler_params=_params("parallel", "parallel"),
        name="conv_gate",
    )(proj, proj, proj, hist, w)


def _ffn_act_kernel(g_ref, u_ref, hist_ref, w_ref, a_ref, nh_ref):
    gate = g_ref[...]
    gc, new_hist = _conv3(gate, hist_ref[0], w_ref[...])
    a_ref[...] = (gc * _sigmoid(gc) * u_ref[...]).astype(a_ref.dtype)
    nh_ref[0] = new_hist


def ffn_act(gate, up, hist, w, *, n_batch, t, d_ff, out_dtype):
    tc = _conv_cols(t, d_ff)
    nb = d_ff // tc
    return pl.pallas_call(
        _ffn_act_kernel,
        grid=(n_batch, nb),
        in_specs=[pl.BlockSpec((t, tc), lambda b, j: (b, j)),
                  pl.BlockSpec((t, tc), lambda b, j: (b, j)),
                  pl.BlockSpec((1, 2, tc), lambda b, j: (b, 0, j)),
                  pl.BlockSpec((3, tc), lambda b, j: (0, j))],
        out_specs=[pl.BlockSpec((t, tc), lambda b, j: (b, j)),
                   pl.BlockSpec((1, 2, tc), lambda b, j: (b, 0, j))],
        out_shape=[jax.ShapeDtypeStruct((n_batch * t, d_ff), out_dtype),
                   jax.ShapeDtypeStruct((n_batch, 2, d_ff), F32)],
        compiler_params=_params("parallel", "parallel"),
        name="ffn_act",
    )(gate, up, hist, w)


def _headnorm_kernel(x_ref, g_ref, o_ref):
    o_ref[...] = _rms(x_ref[...], g_ref[...])


def headnorm(src, col0, n_heads, gain):
    m = src.shape[0]
    tr = min(m, 512)
    cb = col0 // HEAD_DIM
    return pl.pallas_call(
        _headnorm_kernel,
        grid=(m // tr, n_heads),
        in_specs=[pl.BlockSpec((tr, HEAD_DIM), lambda i, h: (i, cb + h)),
                  pl.BlockSpec((1, HEAD_DIM), lambda i, h: (0, 0))],
        out_specs=pl.BlockSpec((tr, HEAD_DIM), lambda i, h: (i, h)),
        out_shape=jax.ShapeDtypeStruct((m, n_heads * HEAD_DIM), F32),
        compiler_params=_params("parallel", "parallel"),
        name="headnorm",
    )(src, gain.reshape(1, HEAD_DIM))


CMP_PARTS = 4
PAGES_PER_STEP = 4


def _page_group(page_ref, g, n_groups, by_rows):
    if by_rows:
        return page_ref[pl.ds(g, PAGE_SIZE, stride=n_groups), :]
    return page_ref[:, g * HEAD_DIM:(g + 1) * HEAD_DIM]


def _compress_kernel(*refs, chunk, n_groups, pps, by_rows):
    k_refs, v_refs = refs[1:1 + pps], refs[1 + pps:1 + 2 * pps]
    pe_ref, wk_ref, wv_ref, g_ref, kc_ref, vc_ref, ks_ref, vs_ref = refs[1 + 2 * pps:]
    steps = chunk // pps
    slot = pl.program_id(1) % steps
    for i in range(pps):
        off = pl.multiple_of((slot * pps + i) * PAGE_SIZE, PAGE_SIZE)
        for g in range(n_groups):
            ks_ref[g, pl.ds(off, PAGE_SIZE), :] = _page_group(k_refs[i], g, n_groups, by_rows) + pe_ref[...]
            vs_ref[g, pl.ds(off, PAGE_SIZE), :] = _page_group(v_refs[i], g, n_groups, by_rows) + pe_ref[...]
    nb = chunk * (PAGE_SIZE // CMP_BLOCK)

    @pl.when(slot == steps - 1)
    def _():
        for s_ref, w_ref, o_ref, norm in ((ks_ref, wk_ref, kc_ref, True), (vs_ref, wv_ref, vc_ref, False)):
            parts = [None] * CMP_PARTS
            for lp in range(CMP_BLOCK // 2):
                a = jnp.concatenate(
                    [jnp.concatenate([s_ref[g, pl.ds(2 * lp + i, nb, stride=CMP_BLOCK), :] for i in range(2)], axis=1)
                     for g in range(n_groups)], axis=0)
                d = _dot(a.astype(BF16), w_ref[lp])
                parts[lp % CMP_PARTS] = d if parts[lp % CMP_PARTS] is None else parts[lp % CMP_PARTS] + d
            acc = functools.reduce(lambda x, y: x + y, parts)
            if norm:
                acc = _rms(acc, g_ref[...])
            for g in range(n_groups):
                o_ref[0, g] = acc[g * nb:(g + 1) * nb]


def compress(src_k, src_v, col_k, col_v, table, pe, wk, wv, gain, *, n_batch, n_pages, chunk, n_groups, by_rows):
    width = n_groups * HEAD_DIM
    nb_total = n_pages * (PAGE_SIZE // CMP_BLOCK)
    nb_chunk = chunk * (PAGE_SIZE // CMP_BLOCK)
    ck, cv = col_k // width, col_v // width
    pe_t = jnp.tile(pe, (PAGE_SIZE // CMP_BLOCK, 1))
    pps = _pick(chunk, (PAGES_PER_STEP, 2, 1))
    steps = chunk // pps
    page_shape = (PAGE_SIZE * n_groups, HEAD_DIM) if by_rows else (PAGE_SIZE, width)

    def page_spec(i, col):
        return pl.BlockSpec(page_shape, lambda b, s, t: (t[b * n_pages + s * pps + i], col))

    out_spec = pl.BlockSpec((1, n_groups, nb_chunk, HEAD_DIM), lambda b, s, t: (b, 0, s // steps, 0))
    w_spec = pl.BlockSpec((CMP_BLOCK // 2, 2 * HEAD_DIM, HEAD_DIM), lambda b, s, t: (0, 0, 0))
    grid_spec = pltpu.PrefetchScalarGridSpec(
        num_scalar_prefetch=1,
        grid=(n_batch, n_pages // pps),
        in_specs=[page_spec(i, ck) for i in range(pps)] + [page_spec(i, cv) for i in range(pps)]
        + [pl.BlockSpec((PAGE_SIZE, HEAD_DIM), lambda b, s, t: (0, 0)), w_spec, w_spec,
           pl.BlockSpec((1, HEAD_DIM), lambda b, s, t: (0, 0))],
        out_specs=[out_spec, out_spec],
        scratch_shapes=[pltpu.VMEM((n_groups, chunk * PAGE_SIZE, HEAD_DIM), F32),
                        pltpu.VMEM((n_groups, chunk * PAGE_SIZE, HEAD_DIM), F32)],
    )
    out = jax.ShapeDtypeStruct((n_batch, n_groups, nb_total, HEAD_DIM), F32)
    return pl.pallas_call(
        functools.partial(_compress_kernel, chunk=chunk, n_groups=n_groups, pps=pps, by_rows=by_rows),
        grid_spec=grid_spec,
        out_shape=[out, out],
        compiler_params=_params("parallel", "arbitrary"),
        name="compress",
    )(table, *([src_k] * pps), *([src_v] * pps), pe_t, wk.reshape(CMP_BLOCK // 2, 2 * HEAD_DIM, HEAD_DIM),
      wv.reshape(CMP_BLOCK // 2, 2 * HEAD_DIM, HEAD_DIM), gain.reshape(1, HEAD_DIM))


N_BIAS_TILES = 3
CMP_NEAR = -(-(MAX_DISTANCE + CMP_BLOCK - 1) // CMP_BLOCK)


def _bias_tiles_kernel(rb_ref, o_ref):
    h = pl.program_id(0)
    i = lax.broadcasted_iota(jnp.int32, (TILE, TILE), 0)
    j = lax.broadcasted_iota(jnp.int32, (TILE, TILE), 1)
    for m in range(2):
        o_ref[0, m] = _bias_from_bucket(_t5_bucket(i - j + TILE * m), rb_ref, h)
    o_ref[0, 2] = _bias_from_bucket(_t5_bucket(i - (CMP_BLOCK - 1) + CMP_BLOCK * (j - 1)), rb_ref, h)


def bias_tiles(rel_bias):
    n_heads = rel_bias.shape[1]
    return pl.pallas_call(
        _bias_tiles_kernel,
        grid=(n_heads,),
        in_specs=[pl.BlockSpec(memory_space=pltpu.SMEM)],
        out_specs=pl.BlockSpec((1, N_BIAS_TILES, TILE, TILE), lambda h: (h, 0, 0, 0)),
        out_shape=jax.ShapeDtypeStruct((n_heads, N_BIAS_TILES, TILE, TILE), F32),
        compiler_params=_params("arbitrary"),
        name="bias_tiles",
    )(rel_bias)


def _select_blocks(imp, q_pos, ns):
    blk = lax.broadcasted_iota(jnp.int32, imp.shape, 1)
    cur = q_pos // CMP_BLOCK
    forced = (blk == 0) | (blk == cur) | (blk == cur - 1)
    score = jnp.where(forced, FORCED_SCORE, jnp.where(blk <= cur, imp, -1.0))
    cnt = jnp.zeros(imp.shape, jnp.int32)
    for i in range(ns):
        ci = score[:, i:i + 1]
        ahead = (ci > score) | ((ci == score) & (blk > i))
        cnt = cnt + jnp.where(ahead, 1, 0)
    return jnp.where((cnt < min(TOP_N, ns)) & (blk < ns), 1.0, 0.0)


def _nsa_prompt_kernel(rb_ref, q_ref, qg_ref, gl_ref, kc_ref, vc_ref, ks_ref, vs_ref, kw_ref, vw_ref, d_ref,
                       o_ref, selt_ref, *, n_cmp, n_kt):
    g = pl.program_id(1)
    qt = pl.program_id(2)
    R = NSA_GROUP
    q = q_ref[...]
    qn = jnp.concatenate([_rms(q[:, r * HEAD_DIM:(r + 1) * HEAD_DIM], qg_ref[...]) for r in range(R)], axis=0)
    qb = qn.astype(BF16)
    gates = _sigmoid(gl_ref[...])
    ii = lax.broadcasted_iota(jnp.int32, (TILE, TILE), 0)
    jj = lax.broadcasted_iota(jnp.int32, (TILE, TILE), 1)
    far = [rb_ref[FAR_BUCKET, g * R + r] for r in range(R)]

    q_pos = qt * TILE + lax.broadcasted_iota(jnp.int32, (TILE, 1), 0)
    c_end = (lax.broadcasted_iota(jnp.int32, (1, n_cmp), 1) + 1) * CMP_BLOCK - 1
    c_dist = q_pos - c_end
    c_blk = lax.broadcasted_iota(jnp.int32, (TILE, n_cmp), 1)
    lc = _dot_nt(qb, kc_ref[...].astype(BF16)) * ATT_SCALE
    vcb = vc_ref[...].astype(BF16)
    imp = jnp.zeros((TILE, n_cmp), F32)
    o_cmp = []
    for r in range(R):
        bias = jnp.full((TILE, n_cmp), far[r], F32)
        for m in range(-1, CMP_NEAR):
            bias = jnp.where(c_blk == qt * (TILE // CMP_BLOCK) - m, d_ref[r, 2][:, m + 1:m + 2], bias)
        l_r = lc[r * TILE:(r + 1) * TILE] + bias
        p_r = _masked_softmax(l_r, c_dist >= 0)
        imp = imp + p_r
        o_cmp.append(_dot(p_r.astype(BF16), vcb))

    selb = _select_blocks(imp, q_pos, n_cmp).astype(BF16)
    bn = lax.broadcasted_iota(jnp.int32, (n_cmp, TILE), 0)
    bj = lax.broadcasted_iota(jnp.int32, (n_cmp, TILE), 1)
    for kt in range(n_kt):
        expand = jnp.where(bn == kt * (TILE // CMP_BLOCK) + bj // CMP_BLOCK, 1.0, 0.0).astype(BF16)
        selt_ref[kt] = _dot(selb, expand)

    def tile_bias(r, rel):
        if isinstance(rel, int):
            return d_ref[r, rel] if rel < 2 else far[r]
        return jnp.where(rel == 0, d_ref[r, 0], jnp.where(rel == 1, d_ref[r, 1], far[r]))

    def sel_body(kt, carry):
        ms, ls, accs = carry
        rel = qt - kt
        start = pl.multiple_of(kt * TILE, TILE)
        k = ks_ref[pl.ds(start, TILE), :].astype(BF16)
        v = vs_ref[pl.ds(start, TILE), :].astype(BF16)
        s = _dot_nt(qb, k) * ATT_SCALE
        mask = (selt_ref[kt] > 0.5) & (ii - jj + TILE * rel >= 0)
        new_m, new_l, new_acc = [], [], []
        for r in range(R):
            s_r = jnp.where(mask, s[r * TILE:(r + 1) * TILE] + tile_bias(r, rel), NEG)
            m_new = jnp.maximum(ms[r], jnp.max(s_r, axis=-1, keepdims=True))
            alpha = jnp.exp(ms[r] - m_new)
            p = jnp.where(mask, jnp.exp(s_r - m_new), 0.0)
            new_m.append(m_new)
            new_l.append(alpha * ls[r] + jnp.sum(p, axis=-1, keepdims=True))
            new_acc.append(alpha * accs[r] + _dot(p.astype(BF16), v))
        return tuple(new_m), tuple(new_l), tuple(new_acc)

    init = (tuple(jnp.full((TILE, 1), NEG, F32) for _ in range(R)),
            tuple(jnp.zeros((TILE, 1), F32) for _ in range(R)),
            tuple(jnp.zeros((TILE, HEAD_DIM), F32) for _ in range(R)))
    _, l_sel, acc_sel = lax.fori_loop(0, qt + 1, sel_body, init)

    n_rel = WINDOW // TILE + 1
    s_tiles, v_tiles, masks = [], [], []
    for rel in range(n_rel):
        kt = qt - rel
        start = pl.multiple_of(jnp.maximum(kt, 0) * TILE, TILE)
        k = kw_ref[pl.ds(start, TILE), :].astype(BF16)
        v_tiles.append(vw_ref[pl.ds(start, TILE), :].astype(BF16))
        s_tiles.append(_dot_nt(qb, k) * ATT_SCALE)
        dist = ii - jj + TILE * rel + jnp.where(kt >= 0, 0, WINDOW)
        masks.append((dist >= 0) & (dist < WINDOW))
    vwin = jnp.concatenate(v_tiles, axis=0)
    wmask = jnp.concatenate([jnp.where(m, 1.0, 0.0) for m in masks], axis=1) > 0.5

    for r in range(R):
        lw = jnp.concatenate([s_tiles[rel][r * TILE:(r + 1) * TILE] + tile_bias(r, rel)
                              for rel in range(n_rel)], axis=1)
        pw = _masked_softmax(lw, wmask)
        o_win = _dot(pw.astype(BF16), vwin)
        o_sel = acc_sel[r] / jnp.maximum(l_sel[r], 1e-30)
        c = (g * R + r) * N_BRANCH
        o = _column(gates, c) * o_cmp[r] + _column(gates, c + 1) * o_sel + _column(gates, c + 2) * o_win
        o_ref[:, r * HEAD_DIM:(r + 1) * HEAD_DIM] = o.astype(o_ref.dtype)


def nsa_prompt(proj, gl, kc, vc, ks, kw, dt, rel_bias, q_gain, *, n_batch, t, n_groups, col_q, col_vs, col_vw):
    n_qt = t // TILE
    n_cmp = kc.shape[2]
    gw = NSA_GROUP * HEAD_DIM
    cq, cvs, cvw = col_q // gw, col_vs // HEAD_DIM, col_vw // HEAD_DIM
    return pl.pallas_call(
        functools.partial(_nsa_prompt_kernel, n_cmp=n_cmp, n_kt=n_qt),
        grid=(n_batch, n_groups, n_qt),
        in_specs=[pl.BlockSpec(memory_space=pltpu.SMEM),
                  pl.BlockSpec((TILE, gw), lambda b, g, i: (b * n_qt + i, cq + g)),
                  pl.BlockSpec((1, HEAD_DIM), lambda b, g, i: (0, 0)),
                  pl.BlockSpec((TILE, LANES), lambda b, g, i: (b * n_qt + i, 0)),
                  pl.BlockSpec((None, None, n_cmp, HEAD_DIM), lambda b, g, i: (b, g, 0, 0)),
                  pl.BlockSpec((None, None, n_cmp, HEAD_DIM), lambda b, g, i: (b, g, 0, 0)),
                  pl.BlockSpec((t, HEAD_DIM), lambda b, g, i: (b, g)),
                  pl.BlockSpec((t, HEAD_DIM), lambda b, g, i: (b, cvs + g)),
                  pl.BlockSpec((t, HEAD_DIM), lambda b, g, i: (b, g)),
                  pl.BlockSpec((t, HEAD_DIM), lambda b, g, i: (b, cvw + g)),
                  pl.BlockSpec((NSA_GROUP, N_BIAS_TILES, TILE, TILE), lambda b, g, i: (g, 0, 0, 0))],
        out_specs=pl.BlockSpec((TILE, gw), lambda b, g, i: (b * n_qt + i, g)),
        out_shape=jax.ShapeDtypeStruct((n_batch * t, n_groups * gw), BF16),
        scratch_shapes=[pltpu.VMEM((n_qt, TILE, TILE), F32)],
        compiler_params=_params("parallel", "parallel", "arbitrary"),
        name="nsa_prompt",
    )(rel_bias, proj, q_gain.reshape(1, HEAD_DIM), gl, kc, vc, ks, proj, kw, proj, dt)


def _nsa_sample_a_kernel(rb_ref, q_ref, qg_ref, gl_ref, kc_ref, vc_ref, wk_ref, wv_ref, kn_ref, vn_ref,
                         qn_ref, part_ref, sel_ref, *, ns, t_new, past_len, w_buf):
    g = pl.program_id(1)
    R = NSA_GROUP
    n_cols = kc_ref.shape[0]
    q = q_ref[...]
    qn = jnp.concatenate([_rms(q[:, r * HEAD_DIM:(r + 1) * HEAD_DIM], qg_ref[...]) for r in range(R)], axis=0)
    qb = qn.astype(BF16)
    gates = _sigmoid(gl_ref[...])
    q_pos = past_len + lax.broadcasted_iota(jnp.int32, (t_new, 1), 0)

    c_end = (lax.broadcasted_iota(jnp.int32, (1, n_cols), 1) + 1) * CMP_BLOCK - 1
    c_dist = q_pos - c_end
    c_bucket = _t5_bucket(c_dist)
    lc = _dot_nt(qb, kc_ref[...].astype(BF16)) * ATT_SCALE
    vcb = vc_ref[...].astype(BF16)
    imp = jnp.zeros((t_new, n_cols), F32)
    o_cmp = []
    for r in range(R):
        l_r = lc[r * t_new:(r + 1) * t_new] + _bias_from_bucket(c_bucket, rb_ref, g * R + r)
        p_r = _masked_softmax(l_r, c_dist >= 0)
        imp = imp + p_r
        o_cmp.append(_dot(p_r.astype(BF16), vcb))

    sel_ref[...] = _select_blocks(imp, q_pos, ns)

    pad = jnp.zeros((TILE - t_new, HEAD_DIM), F32)
    kwin = jnp.concatenate([wk_ref[...], kn_ref[...], pad], axis=0).astype(BF16)
    vwin = jnp.concatenate([wv_ref[...], vn_ref[...], pad], axis=0).astype(BF16)
    nw = w_buf + TILE
    col = lax.broadcasted_iota(jnp.int32, (1, nw), 1)
    w_pos = past_len - w_buf + col
    w_dist = q_pos - w_pos
    w_mask = (w_dist >= 0) & (w_dist < WINDOW) & (w_pos >= 0) & (col < w_buf + t_new)
    w_bucket = _t5_bucket(w_dist)
    lw = _dot_nt(qb, kwin) * ATT_SCALE
    for r in range(R):
        l_r = lw[r * t_new:(r + 1) * t_new] + _bias_from_bucket(w_bucket, rb_ref, g * R + r)
        o_win = _dot(_masked_softmax(l_r, w_mask).astype(BF16), vwin)
        c = (g * R + r) * N_BRANCH
        part_ref[:, r * HEAD_DIM:(r + 1) * HEAD_DIM] = _column(gates, c) * o_cmp[r] + _column(gates, c + 2) * o_win
        qn_ref[:, r * HEAD_DIM:(r + 1) * HEAD_DIM] = qn[r * t_new:(r + 1) * t_new]


def nsa_sample_a(proj, gl, kc, vc, win_k, win_v, kw_new, rel_bias, q_gain, *, n_batch, t_new, n_groups, past_len,
                 col_q, col_vw):
    ns_pad = kc.shape[2]
    ns = -(-(past_len + t_new) // CMP_BLOCK)
    w_buf = win_k.shape[1]
    gw = NSA_GROUP * HEAD_DIM
    cq, cvw = col_q // gw, col_vw // HEAD_DIM
    d_q = n_groups * gw
    return pl.pallas_call(
        functools.partial(_nsa_sample_a_kernel, ns=ns, t_new=t_new, past_len=past_len, w_buf=w_buf),
        grid=(n_batch, n_groups),
        in_specs=[pl.BlockSpec(memory_space=pltpu.SMEM),
                  pl.BlockSpec((t_new, gw), lambda b, g: (b, cq + g)),
                  pl.BlockSpec((1, HEAD_DIM), lambda b, g: (0, 0)),
                  pl.BlockSpec((t_new, LANES), lambda b, g: (b, 0)),
                  pl.BlockSpec((None, None, ns_pad, HEAD_DIM), lambda b, g: (b, g, 0, 0)),
                  pl.BlockSpec((None, None, ns_pad, HEAD_DIM), lambda b, g: (b, g, 0, 0)),
                  pl.BlockSpec((None, w_buf, HEAD_DIM), lambda b, g: (b, 0, g)),
                  pl.BlockSpec((None, w_buf, HEAD_DIM), lambda b, g: (b, 0, g)),
                  pl.BlockSpec((t_new, HEAD_DIM), lambda b, g: (b, g)),
                  pl.BlockSpec((t_new, HEAD_DIM), lambda b, g: (b, cvw + g))],
        out_specs=[pl.BlockSpec((t_new, gw), lambda b, g: (b, g)),
                   pl.BlockSpec((t_new, gw), lambda b, g: (b, g)),
                   pl.BlockSpec((None, None, t_new, ns_pad), lambda b, g: (b, g, 0, 0))],
        out_shape=[jax.ShapeDtypeStruct((n_batch * t_new, d_q), F32),
                   jax.ShapeDtypeStruct((n_batch * t_new, d_q), F32),
                   jax.ShapeDtypeStruct((n_batch, n_groups, t_new, ns_pad), F32)],
        compiler_params=_params("parallel", "parallel"),
        name="nsa_sample_a",
    )(rel_bias, proj, q_gain.reshape(1, HEAD_DIM), gl, kc, vc, win_k, win_v, kw_new, proj)


def _nsa_sample_b_kernel(*refs, n_pages, n_groups, t_new, past_len, pps):
    rb_ref, qn_ref, gl_ref, part_ref, mask_ref, nmask_ref = refs[1:7]
    k_refs, v_refs = refs[7:7 + pps], refs[7 + pps:7 + 2 * pps]
    kn_ref, vn_ref, o_ref, m_ref, l_ref, acc_ref, bias_ref = refs[7 + 2 * pps:]
    p = pl.program_id(1)
    n_steps = n_pages // pps
    R = NSA_GROUP
    rows_g = R * t_new
    rows = n_groups * rows_g
    row_head = lax.broadcasted_iota(jnp.int32, (rows, 1), 0) // t_new
    n_heads = n_groups * R

    def q_rows(g):
        qn = qn_ref[...]
        return jnp.concatenate([qn[:, (g * R + r) * HEAD_DIM:(g * R + r + 1) * HEAD_DIM] for r in range(R)],
                               axis=0).astype(BF16)

    def near_bias(base):
        qi = lax.broadcasted_iota(jnp.int32, (t_new, TILE), 0)
        kj = lax.broadcasted_iota(jnp.int32, (t_new, TILE), 1)
        bucket = _t5_bucket(qi - kj + base)
        return jnp.concatenate([_bias_from_bucket(bucket, rb_ref, h) for h in range(n_heads)], axis=0)

    @pl.when(p == 0)
    def _():
        m_ref[...] = jnp.full(m_ref.shape, NEG, F32)
        l_ref[...] = jnp.zeros(l_ref.shape, F32)
        acc_ref[...] = jnp.zeros(acc_ref.shape, F32)
        far = jnp.zeros((rows, 1), F32)
        for h in range(n_heads):
            far = jnp.where(row_head == h, rb_ref[FAR_BUCKET, h], far)
        bias_ref[...] = jnp.broadcast_to(far, bias_ref.shape)

    @pl.when(p == n_steps - 1)
    def _():
        bias_ref[:, (pps - 1) * PAGE_SIZE:] = near_bias(past_len - (n_pages - 1) * PAGE_SIZE)

    def update(k_of, v_of, mask_f, bias):
        s = jnp.concatenate([_dot_nt(q_rows(g), k_of(g)) for g in range(n_groups)], axis=0) * ATT_SCALE + bias
        mask = jnp.concatenate([jnp.concatenate([mask_f(g)] * R, axis=0) for g in range(n_groups)], axis=0) > 0.5
        s = jnp.where(mask, s, NEG)
        m_old = m_ref[...]
        m_new = jnp.maximum(m_old, jnp.max(s, axis=-1, keepdims=True))
        alpha = jnp.exp(m_old - m_new)
        pr = jnp.where(mask, jnp.exp(s - m_new), 0.0)
        l_ref[...] = alpha * l_ref[...] + jnp.sum(pr, axis=-1, keepdims=True)
        prb = pr.astype(BF16)
        pv = jnp.concatenate([_dot(prb[g * rows_g:(g + 1) * rows_g], v_of(g)) for g in range(n_groups)], axis=0)
        acc_ref[...] = alpha * acc_ref[...] + pv
        m_ref[...] = m_new

    def pages_of(page_refs):
        return lambda g: jnp.concatenate([_page_group(r, g, n_groups, True) for r in page_refs], axis=0).astype(BF16)

    update(pages_of(k_refs), pages_of(v_refs), lambda g: mask_ref[g], bias_ref[...])

    @pl.when(p == n_steps - 1)
    def _():
        qi = lax.broadcasted_iota(jnp.int32, (t_new, TILE), 0)
        kj = lax.broadcasted_iota(jnp.int32, (t_new, TILE), 1)
        causal = (qi - kj >= 0) & (kj < t_new)
        pad = jnp.zeros((TILE - t_new, n_groups * HEAD_DIM), F32)
        k_new = jnp.concatenate([kn_ref[...], pad], axis=0).astype(BF16)
        v_new = jnp.concatenate([vn_ref[...], pad], axis=0).astype(BF16)
        update(lambda g: k_new[:, g * HEAD_DIM:(g + 1) * HEAD_DIM], lambda g: v_new[:, g * HEAD_DIM:(g + 1) * HEAD_DIM],
               lambda g: jnp.where(causal, nmask_ref[g], 0.0), near_bias(0))
        o_sel = acc_ref[...] / jnp.maximum(l_ref[...], 1e-30)
        gates = _sigmoid(gl_ref[...])
        part = part_ref[...]
        for h in range(n_heads):
            gs = gates[:, h * N_BRANCH + 1:h * N_BRANCH + 2]
            o = part[:, h * HEAD_DIM:(h + 1) * HEAD_DIM] + gs * o_sel[h * t_new:(h + 1) * t_new]
            o_ref[:, h * HEAD_DIM:(h + 1) * HEAD_DIM] = o


def nsa_sample_b(table, rel_bias, qn, gl, part, mask, nmask, cache_k, cache_v, ks_new, proj, *, n_batch, t_new,
                 n_groups, n_pages, past_len, col_vs):
    width = n_groups * HEAD_DIM
    d_q = qn.shape[1]
    rows = n_groups * NSA_GROUP * t_new
    cvs = col_vs // width
    pps = _pick(n_pages, (PAGES_PER_STEP, 2, 1))

    def page_spec(i):
        return pl.BlockSpec((PAGE_SIZE * n_groups, HEAD_DIM), lambda b, s, t: (t[b * n_pages + s * pps + i], 0))

    grid_spec = pltpu.PrefetchScalarGridSpec(
        num_scalar_prefetch=1,
        grid=(n_batch, n_pages // pps),
        in_specs=[pl.BlockSpec(memory_space=pltpu.SMEM),
                  pl.BlockSpec((t_new, d_q), lambda b, s, t: (b, 0)),
                  pl.BlockSpec((t_new, LANES), lambda b, s, t: (b, 0)),
                  pl.BlockSpec((t_new, d_q), lambda b, s, t: (b, 0)),
                  pl.BlockSpec((None, n_groups, t_new, pps * PAGE_SIZE), lambda b, s, t: (b, 0, 0, s)),
                  pl.BlockSpec((None, n_groups, t_new, LANES), lambda b, s, t: (b, 0, 0, 0))]
        + [page_spec(i) for i in range(pps)] * 2
        + [pl.BlockSpec((t_new, width), lambda b, s, t: (b, 0)),
           pl.BlockSpec((t_new, width), lambda b, s, t: (b, cvs))],
        out_specs=pl.BlockSpec((t_new, d_q), lambda b, s, t: (b, 0)),
        scratch_shapes=[pltpu.VMEM((rows, 1), F32), pltpu.VMEM((rows, 1), F32),
                        pltpu.VMEM((rows, HEAD_DIM), F32), pltpu.VMEM((rows, pps * PAGE_SIZE), F32)],
    )
    return pl.pallas_call(
        functools.partial(_nsa_sample_b_kernel, n_pages=n_pages, n_groups=n_groups, t_new=t_new, past_len=past_len,
                          pps=pps),
        grid_spec=grid_spec,
        out_shape=jax.ShapeDtypeStruct((n_batch * t_new, d_q), F32),
        compiler_params=_params("parallel", "arbitrary"),
        name="nsa_sample_b",
    )(table, rel_bias, qn, gl, part, mask, nmask, *([cache_k] * pps), *([cache_v] * pps), ks_new, proj)


def _sb_tile(z, valid, carry, upper):
    return _sb_tiles([z], valid, [carry], upper, one_product=False)[0]


def _sb_tiles(zs, valid, carries, upper, one_product):
    rows = zs[0].shape[0]
    w = upper.shape[0]
    n = zs[0].shape[1] // w
    if valid is None:
        keep = lambda x, cols=slice(None): x
    elif valid.dtype == F32:
        keep = lambda x, cols=slice(None): x * valid[:, cols]
    else:
        keep = lambda x, cols=slice(None): jnp.where(valid[:, cols], x, 0.0)
    lss = [_log_sigmoid(z) for z in zs]
    lks = [keep(ls - z) for ls, z in zip(lss, zs)]
    parts = []
    for lk in lks:
        for c in range(n):
            lkc = lk[:, c * w:(c + 1) * w]
            hi = lkc.astype(BF16)
            parts += [hi, (lkc - hi.astype(F32)).astype(BF16)]
    if one_product:
        both = _dot(jnp.concatenate(parts, axis=0), upper)
        pair = lambda i: both[2 * i * rows:(2 * i + 2) * rows]
    else:
        pair = lambda i: _dot(jnp.concatenate(parts[2 * i:2 * i + 2], axis=0), upper)
    out = []
    for g, (ls, lk, carry) in enumerate(zip(lss, lks, carries)):
        chunks = [None] * n
        for c in reversed(range(n)):
            cols = slice(c * w, (c + 1) * w)
            hi_lo = pair(g * n + c)
            suffix = hi_lo[:rows] + hi_lo[rows:]
            chunks[c] = keep(jnp.exp(ls[:, cols] + suffix + carry), cols)
            carry = carry + jnp.sum(lk[:, cols], axis=-1, keepdims=True)
        out.append((chunks[0] if n == 1 else jnp.concatenate(chunks, axis=1), carry))
    return out


def _strict_upper(n):
    j = lax.broadcasted_iota(jnp.int32, (n, n), 0)
    s = lax.broadcasted_iota(jnp.int32, (n, n), 1)
    return jnp.where(j > s, 1.0, 0.0).astype(BF16)


SB_TQ = 512
SB_TK = 256


def _sb_prompt_kernel(q_ref, k_ref, v_ref, o_ref, *, tq, tk):
    qi = pl.program_id(2)
    qb = q_ref[...].astype(BF16)
    upper = _strict_upper(tk)
    n_diag = tq // tk

    def tile(start, q_rows, valid, carry, acc, width=tk):
        k = k_ref[pl.ds(start, width), :].astype(BF16)
        v = v_ref[pl.ds(start, width), :].astype(BF16)
        z = _dot_nt(q_rows, k) * ATT_SCALE
        a, carry = _sb_tile(z, valid, carry, upper)
        return carry, acc + _dot(a.astype(BF16), v)

    carry = jnp.zeros((tq, 1), F32)
    acc = jnp.zeros((tq, HEAD_DIM), F32)
    for d in reversed(range(n_diag)):
        r0 = d * tk
        ii = lax.broadcasted_iota(jnp.int32, (tq - r0, tk), 0)
        jj = lax.broadcasted_iota(jnp.int32, (tq - r0, tk), 1)
        start = pl.multiple_of(qi * tq + r0, tk)
        c_new, a_new = tile(start, qb[r0:], jj < ii, carry[r0:], acc[r0:])
        carry = c_new if r0 == 0 else jnp.concatenate([carry[:r0], c_new], axis=0)
        acc = a_new if r0 == 0 else jnp.concatenate([acc[:r0], a_new], axis=0)

    def body(n, c):
        start = pl.multiple_of((qi - 1 - n) * tq, tq)
        return tile(start, qb, None, *c, width=tq)

    _, acc = lax.fori_loop(0, qi, body, (carry, acc))
    o_ref[...] = acc.astype(o_ref.dtype)


def sb_prompt(q, k, v, *, n_batch, t, n_heads):
    tq = _pick(t, (SB_TQ, SB_TK))
    tk = SB_TK
    n_qt = t // tq
    return pl.pallas_call(
        functools.partial(_sb_prompt_kernel, tq=tq, tk=tk),
        grid=(n_batch, n_heads, n_qt),
        in_specs=[pl.BlockSpec((tq, HEAD_DIM), lambda b, h, i: (b * n_qt + i, h)),
                  pl.BlockSpec((t, HEAD_DIM), lambda b, h, i: (b, h)),
                  pl.BlockSpec((t, HEAD_DIM), lambda b, h, i: (b, h))],
        out_specs=pl.BlockSpec((tq, HEAD_DIM), lambda b, h, i: (b * n_qt + i, h)),
        out_shape=jax.ShapeDtypeStruct((n_batch * t, n_heads * HEAD_DIM), BF16),
        compiler_params=_params("parallel", "parallel", "arbitrary"),
        name="sb_prompt",
    )(q, k, v)


SUBLANES = 8


def _sb_sample_kernel(tbl_ref, q_ref, kn_ref, vn_ref, k_ref, v_ref, o_ref, qs_ref, c_ref, acc_ref, *,
                      n_pages, n_heads, t_new):
    del tbl_ref
    p = pl.program_id(1)
    upper = _strict_upper(SB_TK)
    hb_rows = SUBLANES * t_new
    n_lanes = PAGE_SIZE * SUBLANES

    @pl.when(p == 0)
    def _():
        q = q_ref[...]
        qs = jnp.concatenate([q[:, h * HEAD_DIM:(h + 1) * HEAD_DIM] for h in range(n_heads)], axis=0)
        qs_ref[...] = qs
        rows = n_heads * t_new
        qi = lax.broadcasted_iota(jnp.int32, (rows, TILE), 0) % t_new
        kj = lax.broadcasted_iota(jnp.int32, (rows, TILE), 1)
        pad = jnp.zeros((TILE - t_new, n_heads * HEAD_DIM), F32)
        kb = jnp.concatenate([kn_ref[...], pad], axis=0).astype(BF16)
        vb = jnp.concatenate([vn_ref[...], pad], axis=0).astype(BF16)
        qsb = qs.astype(BF16)
        z = jnp.concatenate([_dot_nt(qsb[h * t_new:(h + 1) * t_new], kb[:, h * HEAD_DIM:(h + 1) * HEAD_DIM])
                             for h in range(n_heads)], axis=0) * ATT_SCALE
        a, carry = _sb_tile(z, kj < qi, jnp.zeros((rows, 1), F32), upper[:TILE, :TILE])
        ab = a.astype(BF16)
        acc_ref[...] = jnp.concatenate([_dot(ab[h * t_new:(h + 1) * t_new], vb[:, h * HEAD_DIM:(h + 1) * HEAD_DIM])
                                        for h in range(n_heads)], axis=0)
        c_ref[...] = carry

    lane = lax.broadcasted_iota(jnp.int32, (hb_rows, n_lanes), 1)
    row = lax.broadcasted_iota(jnp.int32, (hb_rows, n_lanes), 0)
    own = jnp.where(lane % SUBLANES == row // t_new, 1.0, 0.0)
    qs_all, c_all = qs_ref[...].astype(BF16), c_ref[...]
    n_hb = n_heads // SUBLANES
    blocks = [slice(hb * hb_rows, (hb + 1) * hb_rows) for hb in range(n_hb)]

    def page_block(ref, hb):
        return ref[:, hb * SUBLANES:(hb + 1) * SUBLANES, :].reshape(n_lanes, HEAD_DIM).astype(BF16)

    zs = [_dot_nt(qs_all[rs], page_block(k_ref, hb)) * ATT_SCALE for hb, rs in enumerate(blocks)]
    tiles = _sb_tiles(zs, own, [c_all[rs] for rs in blocks], upper, one_product=True)
    acc_ref[...] += jnp.concatenate([_dot(a.astype(BF16), page_block(v_ref, hb)) for hb, (a, _) in enumerate(tiles)],
                                    axis=0)
    c_ref[...] = jnp.concatenate([carry for _, carry in tiles], axis=0)

    @pl.when(p == n_pages - 1)
    def _():
        acc = acc_ref[...]
        for h in range(n_heads):
            o_ref[:, h * HEAD_DIM:(h + 1) * HEAD_DIM] = acc[h * t_new:(h + 1) * t_new]


def sb_sample(table, q, k, v, cache_k, cache_v, layer, *, n_batch, t_new, n_heads, n_pages):
    d = n_heads * HEAD_DIM
    rows = n_heads * t_new
    assert n_heads % SUBLANES == 0 and (PAGE_SIZE * SUBLANES) % SB_TK == 0
    page = lambda b, p, t: (layer, t[b * n_pages + n_pages - 1 - p], 0, 0, 0)
    grid_spec = pltpu.PrefetchScalarGridSpec(
        num_scalar_prefetch=1,
        grid=(n_batch, n_pages),
        in_specs=[pl.BlockSpec((t_new, d), lambda b, p, t: (b, 0)),
                  pl.BlockSpec((t_new, d), lambda b, p, t: (b, 0)),
                  pl.BlockSpec((t_new, d), lambda b, p, t: (b, 0)),
                  pl.BlockSpec((None, None, PAGE_SIZE, n_heads, HEAD_DIM), page),
                  pl.BlockSpec((None, None, PAGE_SIZE, n_heads, HEAD_DIM), page)],
        out_specs=pl.BlockSpec((t_new, d), lambda b, p, t: (b, 0)),
        scratch_shapes=[pltpu.VMEM((rows, HEAD_DIM), F32), pltpu.VMEM((rows, 1), F32),
                        pltpu.VMEM((rows, HEAD_DIM), F32)],
    )
    return pl.pallas_call(
        functools.partial(_sb_sample_kernel, n_pages=n_pages, n_heads=n_heads, t_new=t_new),
        grid_spec=grid_spec,
        out_shape=jax.ShapeDtypeStruct((n_batch * t_new, d), F32),
        compiler_params=_params("parallel", "arbitrary"),
        name="sb_sample",
    )(table, q, k, v, cache_k, cache_v)


def _pick(n, candidates):
    for c in candidates:
        if n % c == 0:
            return c
    return n


def _mm(xs, w, layer, res=None, nk=1, col_off=0, n_cols=None, name="matmul"):
    m = xs[0].shape[0]
    n = w.shape[2] - col_off if n_cols is None else n_cols
    tm = _pick(m, (1024, 512, 256, 128))
    tn = _pick(n, (512, 256, 128))
    return matmul(xs, w, res, tm=tm, tn=tn, nk=nk, layer=layer, col_off=col_off, n_cols=n_cols, name=name)


def _proj(xs, side_xs, ws, layer, res=None, side_res=None, col_off=0, n_cols=None, name="project"):
    m, ms = xs[0].shape[0], side_xs[0].shape[0]
    k, n_full = ws[0].shape[1:]
    n = n_full - col_off if n_cols is None else n_cols
    w_bytes = ws[0].dtype.itemsize
    for tm in (2048, 1024, 512, 256, 128, m):
        if m % tm:
            continue
        for tn in (512, 256, 128):
            if n % tn or col_off % tn:
                continue
            w_tile = k * tn * (2 * w_bytes + (2 if w_bytes == 4 else 0))
            out_tile = (tm + ms) * tn * 4 * 2 * (2 if res is not None else 1)
            need = tm * k * 2 + 2 * ms * k * 2 + len(ws) * (w_tile + out_tile)
            if need <= PROJECT_VMEM_BUDGET:
                return project(xs, side_xs, ws, layer, res, side_res, tm=tm, tn=tn, col_off=col_off, n_cols=n_cols,
                               name=name)
    raise ValueError("no tile of project() fits the VMEM budget")


def kernel(x_prompt, x_sample, state_conv_a, cache_cmp_k, cache_cmp_v, cache_sel_k, cache_sel_v, state_win_k,
           state_win_v, cache_sb_k, cache_sb_v, state_ffn_conv, page_table, norm_mix, norm_ffn, w_in_even, conv_a_w,
           q_norm, k_norm, w_cmp_k, w_cmp_v, pe_cmp, rel_bias, w_out_even, w_qkv_odd, w_out_odd, w_ffn_gate,
           w_ffn_up, ffn_conv_w, w_ffn_down):
    bp, seq, d_model = x_prompt.shape
    bs, t_new, _ = x_sample.shape
    depth = norm_mix.shape[0]
    n_pages = page_table.shape[1]
    past_len = n_pages * PAGE_SIZE
    d_conv = conv_a_w.shape[2]
    n_heads_nsa = rel_bias.shape[1]
    n_groups = n_heads_nsa // NSA_GROUP
    d_q = n_heads_nsa * HEAD_DIM
    d_kv = n_groups * HEAD_DIM
    sb_heads = w_out_odd.shape[1] // HEAD_DIM
    d_sb = sb_heads * HEAD_DIM
    d_ff = w_ffn_gate.shape[2]
    n_gate_cols = N_BRANCH * n_heads_nsa
    col_q = 3 * d_conv
    col_kv = col_q + d_q
    col_gl = col_kv + 2 * N_BRANCH * d_kv
    assert n_gate_cols <= LANES and seq % TILE == 0 and seq >= WINDOW and t_new <= CMP_BLOCK

    xp = x_prompt.reshape(bp * seq, d_model)
    xs = x_sample.reshape(bs * t_new, d_model)
    table = page_table.reshape(-1).astype(jnp.int32)
    pages_p = seq // PAGE_SIZE
    table_p = jnp.arange(bp * pages_p, dtype=jnp.int32)
    zeros_conv_p = jnp.zeros((bp, 2, d_conv), F32)
    zeros_ffn_p = jnp.zeros((bp, 2, d_ff), F32)
    dt = bias_tiles(rel_bias)

    w_gl_b = jnp.pad(w_in_even[:, :, col_gl:], ((0, 0), (0, 0), (0, LANES - n_gate_cols))).astype(BF16)
    w_down_b = w_ffn_down.astype(BF16)

    outs = {k: [] for k in ("conv_p", "conv_s", "cmp_kp", "cmp_vp", "cmp_ks", "cmp_vs", "sel_kp", "sel_vp", "sel_ks",
                            "sel_vs", "win_kp", "win_vp", "win_ks", "win_vs", "sb_kp", "sb_vp", "sb_ks", "sb_vs",
                            "ffn_p", "ffn_s")}

    def kv4(a, b, t):
        return a.reshape(b, t, n_groups, HEAD_DIM)

    for layer in range(depth):
        hp = rmsnorm_bf16(xp, norm_mix[layer])
        hs = rmsnorm_bf16(xs, norm_mix[layer])
        if layer % 2 == 0:
            e = layer // 2
            wk, wv = w_cmp_k[e].astype(BF16), w_cmp_v[e].astype(BF16)
            cols = [col_kv + i * d_kv for i in range(2 * N_BRANCH)]

            (proj, proj_s) = _proj([hp], [hs], [w_in_even], e, n_cols=col_gl, name="in_proj")
            (gl, gl_s) = _proj([hp], [hs], [w_gl_b], e, name="gate_proj")

            y_a, hist_p = conv_gate(proj, zeros_conv_p, conv_a_w[e], n_batch=bp, t=seq, d_conv=d_conv,
                                    out_dtype=BF16)
            ks_n = headnorm(proj, cols[2], n_groups, k_norm[e, 1])
            kw_n = headnorm(proj, cols[4], n_groups, k_norm[e, 2])
            kc, vc = compress(proj, proj, cols[0], cols[1], table_p, pe_cmp[e], wk, wv, k_norm[e, 0],
                              n_batch=bp, n_pages=pages_p, chunk=pages_p, n_groups=n_groups, by_rows=False)
            o_p = nsa_prompt(proj, gl, kc, vc, ks_n, kw_n, dt, rel_bias, q_norm[e], n_batch=bp, t=seq,
                             n_groups=n_groups, col_q=col_q, col_vs=cols[3], col_vw=cols[5])
            y_a_p = y_a
            outs["conv_p"].append(hist_p)
            outs["cmp_kp"].append(kv4(proj[:, cols[0]:cols[0] + d_kv], bp, seq))
            outs["cmp_vp"].append(kv4(proj[:, cols[1]:cols[1] + d_kv], bp, seq))
            outs["sel_kp"].append(kv4(ks_n, bp, seq))
            outs["sel_vp"].append(kv4(proj[:, cols[3]:cols[3] + d_kv], bp, seq))
            n_keep = min(WINDOW, seq)
            outs["win_kp"].append(kv4(kw_n, bp, seq)[:, seq - n_keep:])
            outs["win_vp"].append(kv4(proj[:, cols[5]:cols[5] + d_kv], bp, seq)[:, seq - n_keep:])

            proj, gl = proj_s, gl_s
            y_a, hist_s = conv_gate(proj, state_conv_a[e], conv_a_w[e], n_batch=bs, t=t_new, d_conv=d_conv,
                                    out_dtype=F32)
            ks_n = headnorm(proj, cols[2], n_groups, k_norm[e, 1])
            kw_n = headnorm(proj, cols[4], n_groups, k_norm[e, 2])
            pool = lambda c: c[e].reshape(-1, HEAD_DIM)
            chunk = _pick(n_pages, (32, 16, 8, 4, 2, 1))
            kc, vc = compress(pool(cache_cmp_k), pool(cache_cmp_v), 0, 0, table, pe_cmp[e], wk, wv, k_norm[e, 0],
                              n_batch=bs, n_pages=n_pages, chunk=chunk, n_groups=n_groups, by_rows=True)
            ns = -(-(past_len + t_new) // CMP_BLOCK)
            pad_rows = ((0, 0), (0, 0), (0, _round_up(ns, LANES) - kc.shape[2]), (0, 0))
            kc, vc = jnp.pad(kc, pad_rows), jnp.pad(vc, pad_rows)
            w_buf = state_win_k.shape[2]
            win_k = state_win_k[e].reshape(bs, w_buf, d_kv)
            win_v = state_win_v[e].reshape(bs, w_buf, d_kv)
            qn, part, sel = nsa_sample_a(proj, gl, kc, vc, win_k, win_v, kw_n, rel_bias, q_norm[e], n_batch=bs,
                                         t_new=t_new, n_groups=n_groups, past_len=past_len, col_q=col_q,
                                         col_vw=cols[5])
            n_past_blocks = past_len // CMP_BLOCK
            mask = jnp.repeat(sel[..., :n_past_blocks], CMP_BLOCK, axis=-1)
            nmask = jnp.broadcast_to(sel[..., n_past_blocks:n_past_blocks + 1], sel.shape[:-1] + (LANES,))
            o = nsa_sample_b(table, rel_bias, qn, gl, part, mask, nmask, pool(cache_sel_k), pool(cache_sel_v), ks_n,
                             proj, n_batch=bs, t_new=t_new, n_groups=n_groups, n_pages=n_pages, past_len=past_len,
                             col_vs=cols[3])
            xp, xs = _proj([y_a_p, o_p], [y_a.astype(BF16), o.astype(BF16)], [w_out_even], e, xp, xs,
                           name="out_proj")
            outs["conv_s"].append(hist_s)
            outs["cmp_ks"].append(kv4(proj[:, cols[0]:cols[0] + d_kv], bs, t_new))
            outs["cmp_vs"].append(kv4(proj[:, cols[1]:cols[1] + d_kv], bs, t_new))
            outs["sel_ks"].append(kv4(ks_n, bs, t_new))
            outs["sel_vs"].append(kv4(proj[:, cols[3]:cols[3] + d_kv], bs, t_new))
            kw_all = jnp.concatenate([state_win_k[e], kv4(kw_n, bs, t_new)], axis=1)
            vw_all = jnp.concatenate([state_win_v[e], kv4(proj[:, cols[5]:cols[5] + d_kv], bs, t_new)], axis=1)
            outs["win_ks"].append(kw_all[:, kw_all.shape[1] - w_buf:])
            outs["win_vs"].append(vw_all[:, vw_all.shape[1] - w_buf:])
        else:
            o_ix = layer // 2
            (q, q_s), (k, k_s), (v, v_s) = (_proj([hp], [hs], [w_qkv_odd], o_ix, col_off=i * d_sb, n_cols=d_sb, name=n)
                                            for i, n in enumerate(("q_proj", "k_proj", "v_proj")))
            o_p = sb_prompt(q, k, v, n_batch=bp, t=seq, n_heads=sb_heads)
            outs["sb_kp"].append(k.reshape(bp, seq, sb_heads, HEAD_DIM))
            outs["sb_vp"].append(v.reshape(bp, seq, sb_heads, HEAD_DIM))
            o_s = sb_sample(table, q_s, k_s, v_s, cache_sb_k, cache_sb_v, o_ix, n_batch=bs, t_new=t_new,
                            n_heads=sb_heads, n_pages=n_pages)
            xp, xs = _proj([o_p], [o_s.astype(BF16)], [w_out_odd], o_ix, xp, xs, name="sb_out")
            outs["sb_ks"].append(k_s.reshape(bs, t_new, sb_heads, HEAD_DIM))
            outs["sb_vs"].append(v_s.reshape(bs, t_new, sb_heads, HEAD_DIM))

        nk_down = 2 if d_ff % (2 * LANES) == 0 else 1
        gate, up, gate_s, up_s = _proj([rmsnorm_bf16(xp, norm_ffn[layer])], [rmsnorm_bf16(xs, norm_ffn[layer])],
                                       [w_ffn_gate, w_ffn_up], layer, name="ffn_gu")
        act, hf_p = ffn_act(gate, up, zeros_ffn_p, ffn_conv_w[layer], n_batch=bp, t=seq, d_ff=d_ff, out_dtype=BF16)
        xp = _mm([act], w_down_b, layer, xp, nk=nk_down, name="ffn_down_p")
        act, hf_s = ffn_act(gate_s, up_s, state_ffn_conv[layer], ffn_conv_w[layer], n_batch=bs, t=t_new, d_ff=d_ff,
                            out_dtype=F32)
        xs = _mm([act.astype(BF16)], w_down_b, layer, xs, nk=nk_down, name="ffn_down_s")
        outs["ffn_p"].append(hf_p)
        outs["ffn_s"].append(hf_s)

    st = lambda key: jnp.stack(outs[key], axis=0)
    return (xp.reshape(bp, seq, d_model), xs.reshape(bs, t_new, d_model), st("conv_p"), st("conv_s"),
            st("cmp_kp"), st("cmp_vp"), st("cmp_ks"), st("cmp_vs"),
            st("sel_kp"), st("sel_vp"), st("sel_ks"), st("sel_vs"),
            st("win_kp"), st("win_vp"), st("win_ks"), st("win_vs"),
            st("sb_kp"), st("sb_vp"), st("sb_ks"), st("sb_vs"),
            st("ffn_p"), st("ffn_s"))
```

```python
import functools
import math

import jax
import jax.numpy as jnp
from jax import lax
from jax.experimental import pallas as pl
from jax.experimental.pallas import tpu as pltpu

F32 = jnp.float32
BF16 = jnp.bfloat16

HEAD_DIM = 128
LANES = 128
PAGE_SIZE = 128
CMP_BLOCK = 64
TOP_N = 16
WINDOW = 512
NSA_GROUP = 4
N_BRANCH = 3
N_BUCKETS = 32
MAX_DISTANCE = 128
MAX_EXACT = N_BUCKETS // 2
FAR_BUCKET = N_BUCKETS - 1
FORCED_SCORE = float(NSA_GROUP + 1)
RMS_EPS = 1e-6
NEG = -1e30
ATT_SCALE = HEAD_DIM ** -0.5
TILE = 128
CONV_TILE_ELEMS = 512 * 1024
VMEM_LIMIT = 56 * 1024 * 1024
PROJECT_VMEM_BUDGET = 48 * 1024 * 1024


def _params(*sem):
    return pltpu.CompilerParams(dimension_semantics=sem, vmem_limit_bytes=VMEM_LIMIT)


def _dot(a, b):
    return jnp.dot(a, b, preferred_element_type=F32)


def _dot_nt(a, b):
    return lax.dot_general(a, b, (((1,), (1,)), ((), ())), preferred_element_type=F32)


def _rms(x, gain):
    return x * lax.rsqrt(jnp.mean(x * x, axis=-1, keepdims=True) + RMS_EPS) * gain


def _sigmoid(x):
    return 1.0 / (1.0 + jnp.exp(-x))


def _t5_bucket(dist):
    n = jnp.maximum(dist, 0)
    nf = jnp.maximum(n, 1).astype(F32)
    large = MAX_EXACT + (jnp.log(nf / MAX_EXACT) / math.log(MAX_DISTANCE / MAX_EXACT)
                         * (N_BUCKETS - MAX_EXACT)).astype(jnp.int32)
    large = jnp.minimum(large, N_BUCKETS - 1)
    return jnp.where(n < MAX_EXACT, n, large)


def _bias_from_bucket(bucket, rb_ref, head):
    out = jnp.zeros(bucket.shape, F32)
    for k in range(N_BUCKETS):
        out = jnp.where(bucket == k, rb_ref[k, head], out)
    return out


def _masked_softmax(l, mask):
    l = jnp.where(mask, l, NEG)
    m = jnp.max(l, axis=-1, keepdims=True)
    e = jnp.where(mask, jnp.exp(l - m), 0.0)
    return e / jnp.maximum(jnp.sum(e, axis=-1, keepdims=True), 1e-30)


def _log_sigmoid(z):
    return jnp.minimum(z, 0.0) - jnp.log(1.0 + jnp.exp(-jnp.abs(z)))


def _column(x, c):
    lane = lax.broadcasted_iota(jnp.int32, x.shape, 1)
    return jnp.sum(jnp.where(lane == c, x, 0.0), axis=1, keepdims=True)


def _round_up(n, m):
    return -(-n // m) * m


def _rmsnorm_kernel(x_ref, g_ref, o_ref):
    o_ref[...] = _rms(x_ref[...], g_ref[...]).astype(o_ref.dtype)


def rmsnorm_bf16(x, gain):
    m, d = x.shape
    tr = min(m, 256)
    return pl.pallas_call(
        _rmsnorm_kernel,
        grid=(m // tr,),
        in_specs=[pl.BlockSpec((tr, d), lambda i: (i, 0)),
                  pl.BlockSpec((1, d), lambda i: (0, 0))],
        out_specs=pl.BlockSpec((tr, d), lambda i: (i, 0)),
        out_shape=jax.ShapeDtypeStruct((m, d), BF16),
        compiler_params=_params("parallel"),
        name="rmsnorm",
    )(x, gain.reshape(1, d))


def _project_kernel(*refs, n_x, k_sizes, n_w, has_res):
    refs = list(refs)
    x_refs, s_refs = refs[:n_x], refs[n_x:2 * n_x]
    w_refs = refs[2 * n_x:2 * n_x + n_w]
    pos = 2 * n_x + n_w
    r_ref, rs_ref = (refs[pos], refs[pos + 1]) if has_res else (None, None)
    pos += 2 * int(has_res)
    o_refs, os_refs = refs[pos:pos + n_w], refs[pos + n_w:pos + 2 * n_w]
    first = pl.program_id(0) == 0
    for w_ref, o_ref, os_ref in zip(w_refs, o_refs, os_refs):
        acc = None
        w_parts = []
        k0 = 0
        for x_ref, ks in zip(x_refs, k_sizes):
            w_parts.append(w_ref[k0:k0 + ks, :].astype(BF16))
            part = _dot(x_ref[...], w_parts[-1])
            acc = part if acc is None else acc + part
            k0 += ks
        o_ref[...] = acc + r_ref[...] if has_res else acc

        @pl.when(first)
        def _(w_parts=w_parts, os_ref=os_ref):
            acc_s = rs_ref[...] if has_res else None
            for s_ref, w in zip(s_refs, w_parts):
                part_s = _dot(s_ref[...], w)
                acc_s = part_s if acc_s is None else acc_s + part_s
            os_ref[0] = acc_s

        @pl.when(jnp.logical_not(first))
        def _(os_ref=os_ref):
            os_ref[...] = jnp.zeros(os_ref.shape, F32)


def project(xs, side_xs, ws, layer, res=None, side_res=None, *, tm, tn, col_off=0, n_cols=None, name):
    m, ms = xs[0].shape[0], side_xs[0].shape[0]
    k_sizes = tuple(x.shape[1] for x in xs)
    k_total, n_full = ws[0].shape[1:]
    n = n_full - col_off if n_cols is None else n_cols
    assert sum(k_sizes) == k_total and m % tm == 0 and n % tn == 0 and col_off % tn == 0
    jb = col_off // tn
    in_specs = [pl.BlockSpec((tm, ks), lambda i, j: (i, 0), pipeline_mode=pl.Buffered(1)) for ks in k_sizes]
    in_specs += [pl.BlockSpec((ms, ks), lambda i, j: (0, 0)) for ks in k_sizes]
    in_specs += [pl.BlockSpec((None, k_total, tn), lambda i, j: (layer, 0, jb + j)) for _ in ws]
    args = list(xs) + list(side_xs) + list(ws)
    if res is not None:
        in_specs += [pl.BlockSpec((tm, tn), lambda i, j: (i, j)), pl.BlockSpec((ms, tn), lambda i, j: (0, j))]
        args += [res, side_res]
    out_specs = [pl.BlockSpec((tm, tn), lambda i, j: (i, j)) for _ in ws]
    out_specs += [pl.BlockSpec((1, ms, tn), lambda i, j: (i, 0, j)) for _ in ws]
    out_shape = [jax.ShapeDtypeStruct((m, n), F32) for _ in ws]
    out_shape += [jax.ShapeDtypeStruct((m // tm, ms, n), F32) for _ in ws]
    outs = pl.pallas_call(
        functools.partial(_project_kernel, n_x=len(xs), k_sizes=k_sizes, n_w=len(ws), has_res=res is not None),
        grid=(m // tm, n // tn),
        in_specs=in_specs,
        out_specs=out_specs,
        out_shape=out_shape,
        compiler_params=_params("parallel", "parallel"),
        name=name,
    )(*args)
    return list(outs[:len(ws)]) + [o[0] for o in outs[len(ws):]]


def _conv3(u, hist, w):
    t = u.shape[0]
    row = lax.broadcasted_iota(jnp.int32, u.shape, 0)
    h0, h1 = hist[0:1], hist[1:2]
    u1 = jnp.where(row == 0, h1, pltpu.roll(u, 1, 0))
    u2 = jnp.where(row == 0, h0, jnp.where(row == 1, h1, pltpu.roll(u, 2, 0)))
    y = w[2:3] * u
    y = y + w[0:1] * u2
    y = y + w[1:2] * u1
    return y, u[t - 2:t]


def _conv_gate_kernel(xa_ref, ba_ref, ca_ref, hist_ref, w_ref, y_ref, nh_ref):
    u = ca_ref[...] * xa_ref[...]
    conv, new_hist = _conv3(u, hist_ref[0], w_ref[...])
    y_ref[...] = (ba_ref[...] * conv).astype(y_ref.dtype)
    nh_ref[0] = new_hist


def _conv_cols(t, width):
    cols = min(width, max(LANES, CONV_TILE_ELEMS // t // LANES * LANES))
    while width % cols:
        cols -= LANES
    return cols


def conv_gate(proj, hist, w, *, n_batch, t, d_conv, out_dtype):
    tc = _conv_cols(t, d_conv)
    nb = d_conv // tc
    return pl.pallas_call(
        _conv_gate_kernel,
        grid=(n_batch, nb),
        in_specs=[pl.BlockSpec((t, tc), lambda b, j: (b, j)),
                  pl.BlockSpec((t, tc), lambda b, j: (b, nb + j)),
                  pl.BlockSpec((t, tc), lambda b, j: (b, 2 * nb + j)),
                  pl.BlockSpec((1, 2, tc), lambda b, j: (b, 0, j)),
                  pl.BlockSpec((3, tc), lambda b, j: (0, j))],
        out_specs=[pl.BlockSpec((t, tc), lambda b, j: (b, j)),
                   pl.BlockSpec((1, 2, tc), lambda b, j: (b, 0, j))],
        out_shape=[jax.ShapeDtypeStruct((n_batch * t, d_conv), out_dtype),
                   jax.ShapeDtypeStruct((n_batch, 2, d_conv), F32)],
        compiler_params=_params("parallel", "parallel"),
        name="conv_gate",
    )(proj, proj, proj, hist, w)


def _ffn_act_kernel(g_ref, u_ref, hist_ref, w_ref, a_ref, nh_ref):
    gate = g_ref[...]
    gc, new_hist = _conv3(gate, hist_ref[0], w_ref[...])
    a_ref[...] = (gc * _sigmoid(gc) * u_ref[...]).astype(a_ref.dtype)
    nh_ref[0] = new_hist


def ffn_act(gate, up, hist, w, *, n_batch, t, d_ff, out_dtype):
    tc = _conv_cols(t, d_ff)
    nb = d_ff // tc
    return pl.pallas_call(
        _ffn_act_kernel,
        grid=(n_batch, nb),
        in_specs=[pl.BlockSpec((t, tc), lambda b, j: (b, j)),
                  pl.BlockSpec((t, tc), lambda b, j: (b, j)),
                  pl.BlockSpec((1, 2, tc), lambda b, j: (b, 0, j)),
                  pl.BlockSpec((3, tc), lambda b, j: (0, j))],
        out_specs=[pl.BlockSpec((t, tc), lambda b, j: (b, j)),
                   pl.BlockSpec((1, 2, tc), lambda b, j: (b, 0, j))],
        out_shape=[jax.ShapeDtypeStruct((n_batch * t, d_ff), out_dtype),
                   jax.ShapeDtypeStruct((n_batch, 2, d_ff), F32)],
        compiler_params=_params("parallel", "parallel"),
        name="ffn_act",
    )(gate, up, hist, w)


def _headnorm_kernel(x_ref, g_ref, o_ref):
    o_ref[...] = _rms(x_ref[...], g_ref[...])


def headnorm(src, col0, n_heads, gain):
    m = src.shape[0]
    tr = min(m, 512)
    cb = col0 // HEAD_DIM
    return pl.pallas_call(
        _headnorm_kernel,
        grid=(m // tr, n_heads),
        in_specs=[pl.BlockSpec((tr, HEAD_DIM), lambda i, h: (i, cb + h)),
                  pl.BlockSpec((1, HEAD_DIM), lambda i, h: (0, 0))],
        out_specs=pl.BlockSpec((tr, HEAD_DIM), lambda i, h: (i, h)),
        out_shape=jax.ShapeDtypeStruct((m, n_heads * HEAD_DIM), F32),
        compiler_params=_params("parallel", "parallel"),
        name="headnorm",
    )(src, gain.reshape(1, HEAD_DIM))


CMP_PARTS = 4
PAGES_PER_STEP = 4


def _page_group(page_ref, g, n_groups, by_rows):
    if by_rows:
        return page_ref[pl.ds(g, PAGE_SIZE, stride=n_groups), :]
    return page_ref[:, g * HEAD_DIM:(g + 1) * HEAD_DIM]


def _compress_kernel(*refs, chunk, n_groups, pps, by_rows):
    k_refs, v_refs = refs[1:1 + pps], refs[1 + pps:1 + 2 * pps]
    pe_ref, wk_ref, wv_ref, g_ref, kc_ref, vc_ref, ks_ref, vs_ref = refs[1 + 2 * pps:]
    steps = chunk // pps
    slot = pl.program_id(1) % steps
    for i in range(pps):
        off = pl.multiple_of((slot * pps + i) * PAGE_SIZE, PAGE_SIZE)
        for g in range(n_groups):
            ks_ref[g, pl.ds(off, PAGE_SIZE), :] = _page_group(k_refs[i], g, n_groups, by_rows) + pe_ref[...]
            vs_ref[g, pl.ds(off, PAGE_SIZE), :] = _page_group(v_refs[i], g, n_groups, by_rows) + pe_ref[...]
    nb = chunk * (PAGE_SIZE // CMP_BLOCK)

    @pl.when(slot == steps - 1)
    def _():
        for s_ref, w_ref, o_ref, norm in ((ks_ref, wk_ref, kc_ref, True), (vs_ref, wv_ref, vc_ref, False)):
            parts = [None] * CMP_PARTS
            for lp in range(CMP_BLOCK // 2):
                a = jnp.concatenate(
                    [jnp.concatenate([s_ref[g, pl.ds(2 * lp + i, nb, stride=CMP_BLOCK), :] for i in range(2)], axis=1)
                     for g in range(n_groups)], axis=0)
                d = _dot(a.astype(BF16), w_ref[lp])
                parts[lp % CMP_PARTS] = d if parts[lp % CMP_PARTS] is None else parts[lp % CMP_PARTS] + d
            acc = functools.reduce(lambda x, y: x + y, parts)
            if norm:
                acc = _rms(acc, g_ref[...])
            for g in range(n_groups):
                o_ref[0, g] = acc[g * nb:(g + 1) * nb]


def compress(src_k, src_v, col_k, col_v, table, pe, wk, wv, gain, *, n_batch, n_pages, chunk, n_groups, by_rows):
    width = n_groups * HEAD_DIM
    nb_total = n_pages * (PAGE_SIZE // CMP_BLOCK)
    nb_chunk = chunk * (PAGE_SIZE // CMP_BLOCK)
    ck, cv = col_k // width, col_v // width
    pe_t = jnp.tile(pe, (PAGE_SIZE // CMP_BLOCK, 1))
    pps = _pick(chunk, (PAGES_PER_STEP, 2, 1))
    steps = chunk // pps
    page_shape = (PAGE_SIZE * n_groups, HEAD_DIM) if by_rows else (PAGE_SIZE, width)

    def page_spec(i, col):
        return pl.BlockSpec(page_shape, lambda b, s, t: (t[b * n_pages + s * pps + i], col))

    out_spec = pl.BlockSpec((1, n_groups, nb_chunk, HEAD_DIM), lambda b, s, t: (b, 0, s // steps, 0))
    w_spec = pl.BlockSpec((CMP_BLOCK // 2, 2 * HEAD_DIM, HEAD_DIM), lambda b, s, t: (0, 0, 0))
    grid_spec = pltpu.PrefetchScalarGridSpec(
        num_scalar_prefetch=1,
        grid=(n_batch, n_pages // pps),
        in_specs=[page_spec(i, ck) for i in range(pps)] + [page_spec(i, cv) for i in range(pps)]
        + [pl.BlockSpec((PAGE_SIZE, HEAD_DIM), lambda b, s, t: (0, 0)), w_spec, w_spec,
           pl.BlockSpec((1, HEAD_DIM), lambda b, s, t: (0, 0))],
        out_specs=[out_spec, out_spec],
        scratch_shapes=[pltpu.VMEM((n_groups, chunk * PAGE_SIZE, HEAD_DIM), F32),
                        pltpu.VMEM((n_groups, chunk * PAGE_SIZE, HEAD_DIM), F32)],
    )
    out = jax.ShapeDtypeStruct((n_batch, n_groups, nb_total, HEAD_DIM), F32)
    return pl.pallas_call(
        functools.partial(_compress_kernel, chunk=chunk, n_groups=n_groups, pps=pps, by_rows=by_rows),
        grid_spec=grid_spec,
        out_shape=[out, out],
        compiler_params=_params("parallel", "arbitrary"),
        name="compress",
    )(table, *([src_k] * pps), *([src_v] * pps), pe_t, wk.reshape(CMP_BLOCK // 2, 2 * HEAD_DIM, HEAD_DIM),
      wv.reshape(CMP_BLOCK // 2, 2 * HEAD_DIM, HEAD_DIM), gain.reshape(1, HEAD_DIM))


N_BIAS_TILES = 3
CMP_NEAR = -(-(MAX_DISTANCE + CMP_BLOCK - 1) // CMP_BLOCK)


def _bias_tiles_kernel(rb_ref, o_ref):
    h = pl.program_id(0)
    i = lax.broadcasted_iota(jnp.int32, (TILE, TILE), 0)
    j = lax.broadcasted_iota(jnp.int32, (TILE, TILE), 1)
    for m in range(2):
        o_ref[0, m] = _bias_from_bucket(_t5_bucket(i - j + TILE * m), rb_ref, h)
    o_ref[0, 2] = _bias_from_bucket(_t5_bucket(i - (CMP_BLOCK - 1) + CMP_BLOCK * (j - 1)), rb_ref, h)


def bias_tiles(rel_bias):
    n_heads = rel_bias.shape[1]
    return pl.pallas_call(
        _bias_tiles_kernel,
        grid=(n_heads,),
        in_specs=[pl.BlockSpec(memory_space=pltpu.SMEM)],
        out_specs=pl.BlockSpec((1, N_BIAS_TILES, TILE, TILE), lambda h: (h, 0, 0, 0)),
        out_shape=jax.ShapeDtypeStruct((n_heads, N_BIAS_TILES, TILE, TILE), F32),
        compiler_params=_params("arbitrary"),
        name="bias_tiles",
    )(rel_bias)


def _select_blocks(imp, q_pos, ns):
    blk = lax.broadcasted_iota(jnp.int32, imp.shape, 1)
    cur = q_pos // CMP_BLOCK
    forced = (blk == 0) | (blk == cur) | (blk == cur - 1)
    score = jnp.where(forced, FORCED_SCORE, jnp.where(blk <= cur, imp, -1.0))
    cnt = jnp.zeros(imp.shape, jnp.int32)
    for i in range(ns):
        ci = score[:, i:i + 1]
        ahead = (ci > score) | ((ci == score) & (blk > i))
        cnt = cnt + jnp.where(ahead, 1, 0)
    return jnp.where((cnt < min(TOP_N, ns)) & (blk < ns), 1.0, 0.0)


def _nsa_prompt_kernel(rb_ref, q_ref, qg_ref, gl_ref, kc_ref, vc_ref, ks_ref, vs_ref, kw_ref, vw_ref, d_ref,
                       o_ref, selt_ref, *, n_cmp, n_kt):
    g = pl.program_id(1)
    qt = pl.program_id(2)
    R = NSA_GROUP
    q = q_ref[...]
    qn = jnp.concatenate([_rms(q[:, r * HEAD_DIM:(r + 1) * HEAD_DIM], qg_ref[...]) for r in range(R)], axis=0)
    qb = qn.astype(BF16)
    gates = _sigmoid(gl_ref[...])
    ii = lax.broadcasted_iota(jnp.int32, (TILE, TILE), 0)
    jj = lax.broadcasted_iota(jnp.int32, (TILE, TILE), 1)
    far = [rb_ref[FAR_BUCKET, g * R + r] for r in range(R)]

    q_pos = qt * TILE + lax.broadcasted_iota(jnp.int32, (TILE, 1), 0)
    c_end = (lax.broadcasted_iota(jnp.int32, (1, n_cmp), 1) + 1) * CMP_BLOCK - 1
    c_dist = q_pos - c_end
    c_blk = lax.broadcasted_iota(jnp.int32, (TILE, n_cmp), 1)
    lc = _dot_nt(qb, kc_ref[...].astype(BF16)) * ATT_SCALE
    vcb = vc_ref[...].astype(BF16)
    imp = jnp.zeros((TILE, n_cmp), F32)
    o_cmp = []
    for r in range(R):
        bias = jnp.full((TILE, n_cmp), far[r], F32)
        for m in range(-1, CMP_NEAR):
            bias = jnp.where(c_blk == qt * (TILE // CMP_BLOCK) - m, d_ref[r, 2][:, m + 1:m + 2], bias)
        l_r = lc[r * TILE:(r + 1) * TILE] + bias
        p_r = _masked_softmax(l_r, c_dist >= 0)
        imp = imp + p_r
        o_cmp.append(_dot(p_r.astype(BF16), vcb))

    selb = _select_blocks(imp, q_pos, n_cmp).astype(BF16)
    bn = lax.broadcasted_iota(jnp.int32, (n_cmp, TILE), 0)
    bj = lax.broadcasted_iota(jnp.int32, (n_cmp, TILE), 1)
    for kt in range(n_kt):
        expand = jnp.where(bn == kt * (TILE // CMP_BLOCK) + bj // CMP_BLOCK, 1.0, 0.0).astype(BF16)
        selt_ref[kt] = _dot(selb, expand)

    def tile_bias(r, rel):
        if isinstance(rel, int):
            return d_ref[r, rel] if rel < 2 else far[r]
        return jnp.where(rel == 0, d_ref[r, 0], jnp.where(rel == 1, d_ref[r, 1], far[r]))

    def sel_body(kt, carry):
        ms, ls, accs = carry
        rel = qt - kt
        start = pl.multiple_of(kt * TILE, TILE)
        k = ks_ref[pl.ds(start, TILE), :].astype(BF16)
        v = vs_ref[pl.ds(start, TILE), :].astype(BF16)
        s = _dot_nt(qb, k) * ATT_SCALE
        mask = (selt_ref[kt] > 0.5) & (ii - jj + TILE * rel >= 0)
        new_m, new_l, new_acc = [], [], []
        for r in range(R):
            s_r = jnp.where(mask, s[r * TILE:(r + 1) * TILE] + tile_bias(r, rel), NEG)
            m_new = jnp.maximum(ms[r], jnp.max(s_r, axis=-1, keepdims=True))
            alpha = jnp.exp(ms[r] - m_new)
            p = jnp.where(mask, jnp.exp(s_r - m_new), 0.0)
            new_m.append(m_new)
            new_l.append(alpha * ls[r] + jnp.sum(p, axis=-1, keepdims=True))
            new_acc.append(alpha * accs[r] + _dot(p.astype(BF16), v))
        return tuple(new_m), tuple(new_l), tuple(new_acc)

    init = (tuple(jnp.full((TILE, 1), NEG, F32) for _ in range(R)),
            tuple(jnp.zeros((TILE, 1), F32) for _ in range(R)),
            tuple(jnp.zeros((TILE, HEAD_DIM), F32) for _ in range(R)))
    _, l_sel, acc_sel = lax.fori_loop(0, qt + 1, sel_body, init)

    n_rel = WINDOW // TILE + 1
    s_tiles, v_tiles, masks = [], [], []
    for rel in range(n_rel):
        kt = qt - rel
        start = pl.multiple_of(jnp.maximum(kt, 0) * TILE, TILE)
        k = kw_ref[pl.ds(start, TILE), :].astype(BF16)
        v_tiles.append(vw_ref[pl.ds(start, TILE), :].astype(BF16))
        s_tiles.append(_dot_nt(qb, k) * ATT_SCALE)
        dist = ii - jj + TILE * rel + jnp.where(kt >= 0, 0, WINDOW)
        masks.append((dist >= 0) & (dist < WINDOW))
    vwin = jnp.concatenate(v_tiles, axis=0)
    wmask = jnp.concatenate([jnp.where(m, 1.0, 0.0) for m in masks], axis=1) > 0.5

    for r in range(R):
        lw = jnp.concatenate([s_tiles[rel][r * TILE:(r + 1) * TILE] + tile_bias(r, rel)
                              for rel in range(n_rel)], axis=1)
        pw = _masked_softmax(lw, wmask)
        o_win = _dot(pw.astype(BF16), vwin)
        o_sel = acc_sel[r] / jnp.maximum(l_sel[r], 1e-30)
        c = (g * R + r) * N_BRANCH
        o = _column(gates, c) * o_cmp[r] + _column(gates, c + 1) * o_sel + _column(gates, c + 2) * o_win
        o_ref[:, r * HEAD_DIM:(r + 1) * HEAD_DIM] = o.astype(o_ref.dtype)


def nsa_prompt(proj, gl, kc, vc, ks, kw, dt, rel_bias, q_gain, *, n_batch, t, n_groups, col_q, col_vs, col_vw):
    n_qt = t // TILE
    n_cmp = kc.shape[2]
    gw = NSA_GROUP * HEAD_DIM
    cq, cvs, cvw = col_q // gw, col_vs // HEAD_DIM, col_vw // HEAD_DIM
    return pl.pallas_call(
        functools.partial(_nsa_prompt_kernel, n_cmp=n_cmp, n_kt=n_qt),
        grid=(n_batch, n_groups, n_qt),
        in_specs=[pl.BlockSpec(memory_space=pltpu.SMEM),
                  pl.BlockSpec((TILE, gw), lambda b, g, i: (b * n_qt + i, cq + g)),
                  pl.BlockSpec((1, HEAD_DIM), lambda b, g, i: (0, 0)),
                  pl.BlockSpec((TILE, LANES), lambda b, g, i: (b * n_qt + i, 0)),
                  pl.BlockSpec((None, None, n_cmp, HEAD_DIM), lambda b, g, i: (b, g, 0, 0)),
                  pl.BlockSpec((None, None, n_cmp, HEAD_DIM), lambda b, g, i: (b, g, 0, 0)),
                  pl.BlockSpec((t, HEAD_DIM), lambda b, g, i: (b, g)),
                  pl.BlockSpec((t, HEAD_DIM), lambda b, g, i: (b, cvs + g)),
                  pl.BlockSpec((t, HEAD_DIM), lambda b, g, i: (b, g)),
                  pl.BlockSpec((t, HEAD_DIM), lambda b, g, i: (b, cvw + g)),
                  pl.BlockSpec((NSA_GROUP, N_BIAS_TILES, TILE, TILE), lambda b, g, i: (g, 0, 0, 0))],
        out_specs=pl.BlockSpec((TILE, gw), lambda b, g, i: (b * n_qt + i, g)),
        out_shape=jax.ShapeDtypeStruct((n_batch * t, n_groups * gw), BF16),
        scratch_shapes=[pltpu.VMEM((n_qt, TILE, TILE), F32)],
        compiler_params=_params("parallel", "parallel", "arbitrary"),
        name="nsa_prompt",
    )(rel_bias, proj, q_gain.reshape(1, HEAD_DIM), gl, kc, vc, ks, proj, kw, proj, dt)


def _nsa_sample_a_kernel(rb_ref, q_ref, qg_ref, gl_ref, kc_ref, vc_ref, wk_ref, wv_ref, kn_ref, vn_ref,
                         qn_ref, part_ref, sel_ref, *, ns, t_new, past_len, w_buf):
    g = pl.program_id(1)
    R = NSA_GROUP
    n_cols = kc_ref.shape[0]
    q = q_ref[...]
    qn = jnp.concatenate([_rms(q[:, r * HEAD_DIM:(r + 1) * HEAD_DIM], qg_ref[...]) for r in range(R)], axis=0)
    qb = qn.astype(BF16)
    gates = _sigmoid(gl_ref[...])
    q_pos = past_len + lax.broadcasted_iota(jnp.int32, (t_new, 1), 0)

    c_end = (lax.broadcasted_iota(jnp.int32, (1, n_cols), 1) + 1) * CMP_BLOCK - 1
    c_dist = q_pos - c_end
    c_bucket = _t5_bucket(c_dist)
    lc = _dot_nt(qb, kc_ref[...].astype(BF16)) * ATT_SCALE
    vcb = vc_ref[...].astype(BF16)
    imp = jnp.zeros((t_new, n_cols), F32)
    o_cmp = []
    for r in range(R):
        l_r = lc[r * t_new:(r + 1) * t_new] + _bias_from_bucket(c_bucket, rb_ref, g * R + r)
        p_r = _masked_softmax(l_r, c_dist >= 0)
        imp = imp + p_r
        o_cmp.append(_dot(p_r.astype(BF16), vcb))

    sel_ref[...] = _select_blocks(imp, q_pos, ns)

    pad = jnp.zeros((TILE - t_new, HEAD_DIM), F32)
    kwin = jnp.concatenate([wk_ref[...], kn_ref[...], pad], axis=0).astype(BF16)
    vwin = jnp.concatenate([wv_ref[...], vn_ref[...], pad], axis=0).astype(BF16)
    nw = w_buf + TILE
    col = lax.broadcasted_iota(jnp.int32, (1, nw), 1)
    w_pos = past_len - w_buf + col
    w_dist = q_pos - w_pos
    w_mask = (w_dist >= 0) & (w_dist < WINDOW) & (w_pos >= 0) & (col < w_buf + t_new)
    w_bucket = _t5_bucket(w_dist)
    lw = _dot_nt(qb, kwin) * ATT_SCALE
    for r in range(R):
        l_r = lw[r * t_new:(r + 1) * t_new] + _bias_from_bucket(w_bucket, rb_ref, g * R + r)
        o_win = _dot(_masked_softmax(l_r, w_mask).astype(BF16), vwin)
        c = (g * R + r) * N_BRANCH
        part_ref[:, r * HEAD_DIM:(r + 1) * HEAD_DIM] = _column(gates, c) * o_cmp[r] + _column(gates, c + 2) * o_win
        qn_ref[:, r * HEAD_DIM:(r + 1) * HEAD_DIM] = qn[r * t_new:(r + 1) * t_new]


def nsa_sample_a(proj, gl, kc, vc, win_k, win_v, kw_new, rel_bias, q_gain, *, n_batch, t_new, n_groups, past_len,
                 col_q, col_vw):
    ns_pad = kc.shape[2]
    ns = -(-(past_len + t_new) // CMP_BLOCK)
    w_buf = win_k.shape[1]
    gw = NSA_GROUP * HEAD_DIM
    cq, cvw = col_q // gw, col_vw // HEAD_DIM
    d_q = n_groups * gw
    return pl.pallas_call(
        functools.partial(_nsa_sample_a_kernel, ns=ns, t_new=t_new, past_len=past_len, w_buf=w_buf),
        grid=(n_batch, n_groups),
        in_specs=[pl.BlockSpec(memory_space=pltpu.SMEM),
                  pl.BlockSpec((t_new, gw), lambda b, g: (b, cq + g)),
                  pl.BlockSpec((1, HEAD_DIM), lambda b, g: (0, 0)),
                  pl.BlockSpec((t_new, LANES), lambda b, g: (b, 0)),
                  pl.BlockSpec((None, None, ns_pad, HEAD_DIM), lambda b, g: (b, g, 0, 0)),
                  pl.BlockSpec((None, None, ns_pad, HEAD_DIM), lambda b, g: (b, g, 0, 0)),
                  pl.BlockSpec((None, w_buf, HEAD_DIM), lambda b, g: (b, 0, g)),
                  pl.BlockSpec((None, w_buf, HEAD_DIM), lambda b, g: (b, 0, g)),
                  pl.BlockSpec((t_new, HEAD_DIM), lambda b, g: (b, g)),
                  pl.BlockSpec((t_new, HEAD_DIM), lambda b, g: (b, cvw + g))],
        out_specs=[pl.BlockSpec((t_new, gw), lambda b, g: (b, g)),
                   pl.BlockSpec((t_new, gw), lambda b, g: (b, g)),
                   pl.BlockSpec((None, None, t_new, ns_pad), lambda b, g: (b, g, 0, 0))],
        out_shape=[jax.ShapeDtypeStruct((n_batch * t_new, d_q), F32),
                   jax.ShapeDtypeStruct((n_batch * t_new, d_q), F32),
                   jax.ShapeDtypeStruct((n_batch, n_groups, t_new, ns_pad), F32)],
        compiler_params=_params("parallel", "parallel"),
        name="nsa_sample_a",
    )(rel_bias, proj, q_gain.reshape(1, HEAD_DIM), gl, kc, vc, win_k, win_v, kw_new, proj)


def _nsa_sample_b_kernel(*refs, n_pages, n_groups, t_new, past_len, pps):
    rb_ref, qn_ref, gl_ref, part_ref, mask_ref, nmask_ref = refs[1:7]
    k_refs, v_refs = refs[7:7 + pps], refs[7 + pps:7 + 2 * pps]
    kn_ref, vn_ref, o_ref, m_ref, l_ref, acc_ref, bias_ref = refs[7 + 2 * pps:]
    p = pl.program_id(1)
    n_steps = n_pages // pps
    R = NSA_GROUP
    rows_g = R * t_new
    rows = n_groups * rows_g
    row_head = lax.broadcasted_iota(jnp.int32, (rows, 1), 0) // t_new
    n_heads = n_groups * R

    def q_rows(g):
        qn = qn_ref[...]
        return jnp.concatenate([qn[:, (g * R + r) * HEAD_DIM:(g * R + r + 1) * HEAD_DIM] for r in range(R)],
                               axis=0).astype(BF16)

    def near_bias(base):
        qi = lax.broadcasted_iota(jnp.int32, (t_new, TILE), 0)
        kj = lax.broadcasted_iota(jnp.int32, (t_new, TILE), 1)
        bucket = _t5_bucket(qi - kj + base)
        return jnp.concatenate([_bias_from_bucket(bucket, rb_ref, h) for h in range(n_heads)], axis=0)

    @pl.when(p == 0)
    def _():
        m_ref[...] = jnp.full(m_ref.shape, NEG, F32)
        l_ref[...] = jnp.zeros(l_ref.shape, F32)
        acc_ref[...] = jnp.zeros(acc_ref.shape, F32)
        far = jnp.zeros((rows, 1), F32)
        for h in range(n_heads):
            far = jnp.where(row_head == h, rb_ref[FAR_BUCKET, h], far)
        bias_ref[...] = jnp.broadcast_to(far, bias_ref.shape)

    @pl.when(p == n_steps - 1)
    def _():
        bias_ref[:, (pps - 1) * PAGE_SIZE:] = near_bias(past_len - (n_pages - 1) * PAGE_SIZE)

    def update(k_of, v_of, mask_f, bias):
        s = jnp.concatenate([_dot_nt(q_rows(g), k_of(g)) for g in range(n_groups)], axis=0) * ATT_SCALE + bias
        mask = jnp.concatenate([jnp.concatenate([mask_f(g)] * R, axis=0) for g in range(n_groups)], axis=0) > 0.5
        s = jnp.where(mask, s, NEG)
        m_old = m_ref[...]
        m_new = jnp.maximum(m_old, jnp.max(s, axis=-1, keepdims=True))
        alpha = jnp.exp(m_old - m_new)
        pr = jnp.where(mask, jnp.exp(s - m_new), 0.0)
        l_ref[...] = alpha * l_ref[...] + jnp.sum(pr, axis=-1, keepdims=True)
        prb = pr.astype(BF16)
        pv = jnp.concatenate([_dot(prb[g * rows_g:(g + 1) * rows_g], v_of(g)) for g in range(n_groups)], axis=0)
        acc_ref[...] = alpha * acc_ref[...] + pv
        m_ref[...] = m_new

    def pages_of(page_refs):
        return lambda g: jnp.concatenate([_page_group(r, g, n_groups, True) for r in page_refs], axis=0).astype(BF16)

    update(pages_of(k_refs), pages_of(v_refs), lambda g: mask_ref[g], bias_ref[...])

    @pl.when(p == n_steps - 1)
    def _():
        qi = lax.broadcasted_iota(jnp.int32, (t_new, TILE), 0)
        kj = lax.broadcasted_iota(jnp.int32, (t_new, TILE), 1)
        causal = (qi - kj >= 0) & (kj < t_new)
        pad = jnp.zeros((TILE - t_new, n_groups * HEAD_DIM), F32)
        k_new = jnp.concatenate([kn_ref[...], pad], axis=0).astype(BF16)
        v_new = jnp.concatenate([vn_ref[...], pad], axis=0).astype(BF16)
        update(lambda g: k_new[:, g * HEAD_DIM:(g + 1) * HEAD_DIM], lambda g: v_new[:, g * HEAD_DIM:(g + 1) * HEAD_DIM],
               lambda g: jnp.where(causal, nmask_ref[g], 0.0), near_bias(0))
        o_sel = acc_ref[...] / jnp.maximum(l_ref[...], 1e-30)
        gates = _sigmoid(gl_ref[...])
        part = part_ref[...]
        for h in range(n_heads):
            gs = gates[:, h * N_BRANCH + 1:h * N_BRANCH + 2]
            o = part[:, h * HEAD_DIM:(h + 1) * HEAD_DIM] + gs * o_sel[h * t_new:(h + 1) * t_new]
            o_ref[:, h * HEAD_DIM:(h + 1) * HEAD_DIM] = o


def nsa_sample_b(table, rel_bias, qn, gl, part, mask, nmask, cache_k, cache_v, ks_new, proj, *, n_batch, t_new,
                 n_groups, n_pages, past_len, col_vs):
    width = n_groups * HEAD_DIM
    d_q = qn.shape[1]
    rows = n_groups * NSA_GROUP * t_new
    cvs = col_vs // width
    pps = _pick(n_pages, (PAGES_PER_STEP, 2, 1))

    def page_spec(i):
        return pl.BlockSpec((PAGE_SIZE * n_groups, HEAD_DIM), lambda b, s, t: (t[b * n_pages + s * pps + i], 0))

    grid_spec = pltpu.PrefetchScalarGridSpec(
        num_scalar_prefetch=1,
        grid=(n_batch, n_pages // pps),
        in_specs=[pl.BlockSpec(memory_space=pltpu.SMEM),
                  pl.BlockSpec((t_new, d_q), lambda b, s, t: (b, 0)),
                  pl.BlockSpec((t_new, LANES), lambda b, s, t: (b, 0)),
                  pl.BlockSpec((t_new, d_q), lambda b, s, t: (b, 0)),
                  pl.BlockSpec((None, n_groups, t_new, pps * PAGE_SIZE), lambda b, s, t: (b, 0, 0, s)),
                  pl.BlockSpec((None, n_groups, t_new, LANES), lambda b, s, t: (b, 0, 0, 0))]
        + [page_spec(i) for i in range(pps)] * 2
        + [pl.BlockSpec((t_new, width), lambda b, s, t: (b, 0)),
           pl.BlockSpec((t_new, width), lambda b, s, t: (b, cvs))],
        out_specs=pl.BlockSpec((t_new, d_q), lambda b, s, t: (b, 0)),
        scratch_shapes=[pltpu.VMEM((rows, 1), F32), pltpu.VMEM((rows, 1), F32),
                        pltpu.VMEM((rows, HEAD_DIM), F32), pltpu.VMEM((rows, pps * PAGE_SIZE), F32)],
    )
    return pl.pallas_call(
        functools.partial(_nsa_sample_b_kernel, n_pages=n_pages, n_groups=n_groups, t_new=t_new, past_len=past_len,
                          pps=pps),
        grid_spec=grid_spec,
        out_shape=jax.ShapeDtypeStruct((n_batch * t_new, d_q), F32),
        compiler_params=_params("parallel", "arbitrary"),
        name="nsa_sample_b",
    )(table, rel_bias, qn, gl, part, mask, nmask, *([cache_k] * pps), *([cache_v] * pps), ks_new, proj)


def _sb_tile(z, valid, carry, upper):
    return _sb_tiles([z], valid, [carry], upper, one_product=False)[0]


def _sb_tiles(zs, valid, carries, upper, one_product):
    rows = zs[0].shape[0]
    w = upper.shape[0]
    n = zs[0].shape[1] // w
    if valid is None:
        keep = lambda x, cols=slice(None): x
    elif valid.dtype == F32:
        keep = lambda x, cols=slice(None): x * valid[:, cols]
    else:
        keep = lambda x, cols=slice(None): jnp.where(valid[:, cols], x, 0.0)
    lss = [_log_sigmoid(z) for z in zs]
    lks = [keep(ls - z) for ls, z in zip(lss, zs)]
    parts = []
    for lk in lks:
        for c in range(n):
            lkc = lk[:, c * w:(c + 1) * w]
            hi = lkc.astype(BF16)
            parts += [hi, (lkc - hi.astype(F32)).astype(BF16)]
    if one_product:
        both = _dot(jnp.concatenate(parts, axis=0), upper)
        pair = lambda i: both[2 * i * rows:(2 * i + 2) * rows]
    else:
        pair = lambda i: _dot(jnp.concatenate(parts[2 * i:2 * i + 2], axis=0), upper)
    out = []
    for g, (ls, lk, carry) in enumerate(zip(lss, lks, carries)):
        chunks = [None] * n
        for c in reversed(range(n)):
            cols = slice(c * w, (c + 1) * w)
            hi_lo = pair(g * n + c)
            suffix = hi_lo[:rows] + hi_lo[rows:]
            chunks[c] = keep(jnp.exp(ls[:, cols] + suffix + carry), cols)
            carry = carry + jnp.sum(lk[:, cols], axis=-1, keepdims=True)
        out.append((chunks[0] if n == 1 else jnp.concatenate(chunks, axis=1), carry))
    return out


def _strict_upper(n):
    j = lax.broadcasted_iota(jnp.int32, (n, n), 0)
    s = lax.broadcasted_iota(jnp.int32, (n, n), 1)
    return jnp.where(j > s, 1.0, 0.0).astype(BF16)


SB_TQ = 512
SB_TK = 256


def _sb_prompt_kernel(q_ref, k_ref, v_ref, o_ref, *, tq, tk):
    qi = pl.program_id(2)
    qb = q_ref[...].astype(BF16)
    upper = _strict_upper(tk)
    n_diag = tq // tk

    def tile(start, q_rows, valid, carry, acc, width=tk):
        k = k_ref[pl.ds(start, width), :].astype(BF16)
        v = v_ref[pl.ds(start, width), :].astype(BF16)
        z = _dot_nt(q_rows, k) * ATT_SCALE
        a, carry = _sb_tile(z, valid, carry, upper)
        return carry, acc + _dot(a.astype(BF16), v)

    carry = jnp.zeros((tq, 1), F32)
    acc = jnp.zeros((tq, HEAD_DIM), F32)
    for d in reversed(range(n_diag)):
        r0 = d * tk
        ii = lax.broadcasted_iota(jnp.int32, (tq - r0, tk), 0)
        jj = lax.broadcasted_iota(jnp.int32, (tq - r0, tk), 1)
        start = pl.multiple_of(qi * tq + r0, tk)
        c_new, a_new = tile(start, qb[r0:], jj < ii, carry[r0:], acc[r0:])
        carry = c_new if r0 == 0 else jnp.concatenate([carry[:r0], c_new], axis=0)
        acc = a_new if r0 == 0 else jnp.concatenate([acc[:r0], a_new], axis=0)

    def body(n, c):
        start = pl.multiple_of((qi - 1 - n) * tq, tq)
        return tile(start, qb, None, *c, width=tq)

    _, acc = lax.fori_loop(0, qi, body, (carry, acc))
    o_ref[...] = acc.astype(o_ref.dtype)


def sb_prompt(q, k, v, *, n_batch, t, n_heads):
    tq = _pick(t, (SB_TQ, SB_TK))
    tk = SB_TK
    n_qt = t // tq
    return pl.pallas_call(
        functools.partial(_sb_prompt_kernel, tq=tq, tk=tk),
        grid=(n_batch, n_heads, n_qt),
        in_specs=[pl.BlockSpec((tq, HEAD_DIM), lambda b, h, i: (b * n_qt + i, h)),
                  pl.BlockSpec((t, HEAD_DIM), lambda b, h, i: (b, h)),
                  pl.BlockSpec((t, HEAD_DIM), lambda b, h, i: (b, h))],
        out_specs=pl.BlockSpec((tq, HEAD_DIM), lambda b, h, i: (b * n_qt + i, h)),
        out_shape=jax.ShapeDtypeStruct((n_batch * t, n_heads * HEAD_DIM), BF16),
        compiler_params=_params("parallel", "parallel", "arbitrary"),
        name="sb_prompt",
    )(q, k, v)


SUBLANES = 8


def _sb_sample_kernel(tbl_ref, q_ref, kn_ref, vn_ref, k_ref, v_ref, o_ref, qs_ref, c_ref, acc_ref, *,
                      n_pages, n_heads, t_new):
    del tbl_ref
    p = pl.program_id(1)
    upper = _strict_upper(SB_TK)
    hb_rows = SUBLANES * t_new
    n_lanes = PAGE_SIZE * SUBLANES

    @pl.when(p == 0)
    def _():
        q = q_ref[...]
        qs = jnp.concatenate([q[:, h * HEAD_DIM:(h + 1) * HEAD_DIM] for h in range(n_heads)], axis=0)
        qs_ref[...] = qs
        rows = n_heads * t_new
        qi = lax.broadcasted_iota(jnp.int32, (rows, TILE), 0) % t_new
        kj = lax.broadcasted_iota(jnp.int32, (rows, TILE), 1)
        pad = jnp.zeros((TILE - t_new, n_heads * HEAD_DIM), F32)
        kb = jnp.concatenate([kn_ref[...], pad], axis=0).astype(BF16)
        vb = jnp.concatenate([vn_ref[...], pad], axis=0).astype(BF16)
        qsb = qs.astype(BF16)
        z = jnp.concatenate([_dot_nt(qsb[h * t_new:(h + 1) * t_new], kb[:, h * HEAD_DIM:(h + 1) * HEAD_DIM])
                             for h in range(n_heads)], axis=0) * ATT_SCALE
        a, carry = _sb_tile(z, kj < qi, jnp.zeros((rows, 1), F32), upper[:TILE, :TILE])
        ab = a.astype(BF16)
        acc_ref[...] = jnp.concatenate([_dot(ab[h * t_new:(h + 1) * t_new], vb[:, h * HEAD_DIM:(h + 1) * HEAD_DIM])
                                        for h in range(n_heads)], axis=0)
        c_ref[...] = carry

    lane = lax.broadcasted_iota(jnp.int32, (hb_rows, n_lanes), 1)
    row = lax.broadcasted_iota(jnp.int32, (hb_rows, n_lanes), 0)
    own = jnp.where(lane % SUBLANES == row // t_new, 1.0, 0.0)
    qs_all, c_all = qs_ref[...].astype(BF16), c_ref[...]
    n_hb = n_heads // SUBLANES
    blocks = [slice(hb * hb_rows, (hb + 1) * hb_rows) for hb in range(n_hb)]

    def page_block(ref, hb):
        return ref[:, hb * SUBLANES:(hb + 1) * SUBLANES, :].reshape(n_lanes, HEAD_DIM).astype(BF16)

    zs = [_dot_nt(qs_all[rs], page_block(k_ref, hb)) * ATT_SCALE for hb, rs in enumerate(blocks)]
    tiles = _sb_tiles(zs, own, [c_all[rs] for rs in blocks], upper, one_product=True)
    acc_ref[...] += jnp.concatenate([_dot(a.astype(BF16), page_block(v_ref, hb)) for hb, (a, _) in enumerate(tiles)],
                                    axis=0)
    c_ref[...] = jnp.concatenate([carry for _, carry in tiles], axis=0)

    @pl.when(p == n_pages - 1)
    def _():
        acc = acc_ref[...]
        for h in range(n_heads):
            o_ref[:, h * HEAD_DIM:(h + 1) * HEAD_DIM] = acc[h * t_new:(h + 1) * t_new]


def sb_sample(table, q, k, v, cache_k, cache_v, layer, *, n_batch, t_new, n_heads, n_pages):
    d = n_heads * HEAD_DIM
    rows = n_heads * t_new
    assert n_heads % SUBLANES == 0 and (PAGE_SIZE * SUBLANES) % SB_TK == 0
    page = lambda b, p, t: (layer, t[b * n_pages + n_pages - 1 - p], 0, 0, 0)
    grid_spec = pltpu.PrefetchScalarGridSpec(
        num_scalar_prefetch=1,
        grid=(n_batch, n_pages),
        in_specs=[pl.BlockSpec((t_new, d), lambda b, p, t: (b, 0)),
                  pl.BlockSpec((t_new, d), lambda b, p, t: (b, 0)),
                  pl.BlockSpec((t_new, d), lambda b, p, t: (b, 0)),
                  pl.BlockSpec((None, None, PAGE_SIZE, n_heads, HEAD_DIM), page),
                  pl.BlockSpec((None, None, PAGE_SIZE, n_heads, HEAD_DIM), page)],
        out_specs=pl.BlockSpec((t_new, d), lambda b, p, t: (b, 0)),
        scratch_shapes=[pltpu.VMEM((rows, HEAD_DIM), F32), pltpu.VMEM((rows, 1), F32),
                        pltpu.VMEM((rows, HEAD_DIM), F32)],
    )
    return pl.pallas_call(
        functools.partial(_sb_sample_kernel, n_pages=n_pages, n_heads=n_heads, t_new=t_new),
        grid_spec=grid_spec,
        out_shape=jax.ShapeDtypeStruct((n_batch * t_new, d), F32),
        compiler_params=_params("parallel", "arbitrary"),
        name="sb_sample",
    )(table, q, k, v, cache_k, cache_v)


def _pick(n, candidates):
    for c in candidates:
        if n % c == 0:
            return c
    return n


def _proj(xs, side_xs, ws, layer, res=None, side_res=None, col_off=0, n_cols=None, name="project"):
    m, ms = xs[0].shape[0], side_xs[0].shape[0]
    k, n_full = ws[0].shape[1:]
    n = n_full - col_off if n_cols is None else n_cols
    w_bytes = ws[0].dtype.itemsize
    for tm in (2048, 1024, 512, 256, 128, m):
        if m % tm:
            continue
        for tn in (512, 256, 128):
            if n % tn or col_off % tn:
                continue
            w_tile = k * tn * (2 * w_bytes + (2 if w_bytes == 4 else 0))
            out_tile = (tm + ms) * tn * 4 * 2 * (2 if res is not None else 1)
            need = tm * k * 2 + 2 * ms * k * 2 + len(ws) * (w_tile + out_tile)
            if need <= PROJECT_VMEM_BUDGET:
                return project(xs, side_xs, ws, layer, res, side_res, tm=tm, tn=tn, col_off=col_off, n_cols=n_cols,
                               name=name)
    raise ValueError("no tile of project() fits the VMEM budget")


def kernel(x_prompt, x_sample, state_conv_a, cache_cmp_k, cache_cmp_v, cache_sel_k, cache_sel_v, state_win_k,
           state_win_v, cache_sb_k, cache_sb_v, state_ffn_conv, page_table, norm_mix, norm_ffn, w_in_even, conv_a_w,
           q_norm, k_norm, w_cmp_k, w_cmp_v, pe_cmp, rel_bias, w_out_even, w_qkv_odd, w_out_odd, w_ffn_gate,
           w_ffn_up, ffn_conv_w, w_ffn_down):
    bp, seq, d_model = x_prompt.shape
    bs, t_new, _ = x_sample.shape
    depth = norm_mix.shape[0]
    n_pages = page_table.shape[1]
    past_len = n_pages * PAGE_SIZE
    d_conv = conv_a_w.shape[2]
    n_heads_nsa = rel_bias.shape[1]
    n_groups = n_heads_nsa // NSA_GROUP
    d_q = n_heads_nsa * HEAD_DIM
    d_kv = n_groups * HEAD_DIM
    sb_heads = w_out_odd.shape[1] // HEAD_DIM
    d_sb = sb_heads * HEAD_DIM
    d_ff = w_ffn_gate.shape[2]
    n_gate_cols = N_BRANCH * n_heads_nsa
    col_q = 3 * d_conv
    col_kv = col_q + d_q
    col_gl = col_kv + 2 * N_BRANCH * d_kv
    assert n_gate_cols <= LANES and seq % TILE == 0 and seq >= WINDOW and t_new <= CMP_BLOCK

    xp = x_prompt.reshape(bp * seq, d_model)
    xs = x_sample.reshape(bs * t_new, d_model)
    table = page_table.reshape(-1).astype(jnp.int32)
    pages_p = seq // PAGE_SIZE
    table_p = jnp.arange(bp * pages_p, dtype=jnp.int32)
    zeros_conv_p = jnp.zeros((bp, 2, d_conv), F32)
    zeros_ffn_p = jnp.zeros((bp, 2, d_ff), F32)
    dt = bias_tiles(rel_bias)

    w_gl_b = jnp.pad(w_in_even[:, :, col_gl:], ((0, 0), (0, 0), (0, LANES - n_gate_cols))).astype(BF16)
    w_down_b = w_ffn_down.astype(BF16)

    outs = {k: [] for k in ("conv_p", "conv_s", "cmp_kp", "cmp_vp", "cmp_ks", "cmp_vs", "sel_kp", "sel_vp", "sel_ks",
                            "sel_vs", "win_kp", "win_vp", "win_ks", "win_vs", "sb_kp", "sb_vp", "sb_ks", "sb_vs",
                            "ffn_p", "ffn_s")}

    def kv4(a, b, t):
        return a.reshape(b, t, n_groups, HEAD_DIM)

    for layer in range(depth):
        hp = rmsnorm_bf16(xp, norm_mix[layer])
        hs = rmsnorm_bf16(xs, norm_mix[layer])
        if layer % 2 == 0:
            e = layer // 2
            wk, wv = w_cmp_k[e].astype(BF16), w_cmp_v[e].astype(BF16)
            cols = [col_kv + i * d_kv for i in range(2 * N_BRANCH)]

            (proj, proj_s) = _proj([hp], [hs], [w_in_even], e, n_cols=col_gl, name="in_proj")
            (gl, gl_s) = _proj([hp], [hs], [w_gl_b], e, name="gate_proj")

            y_a, hist_p = conv_gate(proj, zeros_conv_p, conv_a_w[e], n_batch=bp, t=seq, d_conv=d_conv,
                                    out_dtype=BF16)
            ks_n = headnorm(proj, cols[2], n_groups, k_norm[e, 1])
            kw_n = headnorm(proj, cols[4], n_groups, k_norm[e, 2])
            kc, vc = compress(proj, proj, cols[0], cols[1], table_p, pe_cmp[e], wk, wv, k_norm[e, 0],
                              n_batch=bp, n_pages=pages_p, chunk=pages_p, n_groups=n_groups, by_rows=False)
            o_p = nsa_prompt(proj, gl, kc, vc, ks_n, kw_n, dt, rel_bias, q_norm[e], n_batch=bp, t=seq,
                             n_groups=n_groups, col_q=col_q, col_vs=cols[3], col_vw=cols[5])
            y_a_p = y_a
            outs["conv_p"].append(hist_p)
            outs["cmp_kp"].append(kv4(proj[:, cols[0]:cols[0] + d_kv], bp, seq))
            outs["cmp_vp"].append(kv4(proj[:, cols[1]:cols[1] + d_kv], bp, seq))
            outs["sel_kp"].append(kv4(ks_n, bp, seq))
            outs["sel_vp"].append(kv4(proj[:, cols[3]:cols[3] + d_kv], bp, seq))
            n_keep = min(WINDOW, seq)
            outs["win_kp"].append(kv4(kw_n, bp, seq)[:, seq - n_keep:])
            outs["win_vp"].append(kv4(proj[:, cols[5]:cols[5] + d_kv], bp, seq)[:, seq - n_keep:])

            proj, gl = proj_s, gl_s
            y_a, hist_s = conv_gate(proj, state_conv_a[e], conv_a_w[e], n_batch=bs, t=t_new, d_conv=d_conv,
                                    out_dtype=F32)
            ks_n = headnorm(proj, cols[2], n_groups, k_norm[e, 1])
            kw_n = headnorm(proj, cols[4], n_groups, k_norm[e, 2])
            pool = lambda c: c[e].reshape(-1, HEAD_DIM)
            chunk = _pick(n_pages, (32, 16, 8, 4, 2, 1))
            kc, vc = compress(pool(cache_cmp_k), pool(cache_cmp_v), 0, 0, table, pe_cmp[e], wk, wv, k_norm[e, 0],
                              n_batch=bs, n_pages=n_pages, chunk=chunk, n_groups=n_groups, by_rows=True)
            ns = -(-(past_len + t_new) // CMP_BLOCK)
            pad_rows = ((0, 0), (0, 0), (0, _round_up(ns, LANES) - kc.shape[2]), (0, 0))
            kc, vc = jnp.pad(kc, pad_rows), jnp.pad(vc, pad_rows)
            w_buf = state_win_k.shape[2]
            win_k = state_win_k[e].reshape(bs, w_buf, d_kv)
            win_v = state_win_v[e].reshape(bs, w_buf, d_kv)
            qn, part, sel = nsa_sample_a(proj, gl, kc, vc, win_k, win_v, kw_n, rel_bias, q_norm[e], n_batch=bs,
                                         t_new=t_new, n_groups=n_groups, past_len=past_len, col_q=col_q,
                                         col_vw=cols[5])
            n_past_blocks = past_len // CMP_BLOCK
            mask = jnp.repeat(sel[..., :n_past_blocks], CMP_BLOCK, axis=-1)
            nmask = jnp.broadcast_to(sel[..., n_past_blocks:n_past_blocks + 1], sel.shape[:-1] + (LANES,))
            o = nsa_sample_b(table, rel_bias, qn, gl, part, mask, nmask, pool(cache_sel_k), pool(cache_sel_v), ks_n,
                             proj, n_batch=bs, t_new=t_new, n_groups=n_groups, n_pages=n_pages, past_len=past_len,
                             col_vs=cols[3])
            xp, xs = _proj([y_a_p, o_p], [y_a.astype(BF16), o.astype(BF16)], [w_out_even], e, xp, xs,
                           name="out_proj")
            outs["conv_s"].append(hist_s)
            outs["cmp_ks"].append(kv4(proj[:, cols[0]:cols[0] + d_kv], bs, t_new))
            outs["cmp_vs"].append(kv4(proj[:, cols[1]:cols[1] + d_kv], bs, t_new))
            outs["sel_ks"].append(kv4(ks_n, bs, t_new))
            outs["sel_vs"].append(kv4(proj[:, cols[3]:cols[3] + d_kv], bs, t_new))
            kw_all = jnp.concatenate([state_win_k[e], kv4(kw_n, bs, t_new)], axis=1)
            vw_all = jnp.concatenate([state_win_v[e], kv4(proj[:, cols[5]:cols[5] + d_kv], bs, t_new)], axis=1)
            outs["win_ks"].append(kw_all[:, kw_all.shape[1] - w_buf:])
            outs["win_vs"].append(vw_all[:, vw_all.shape[1] - w_buf:])
        else:
            o_ix = layer // 2
            (q, q_s), (k, k_s), (v, v_s) = (_proj([hp], [hs], [w_qkv_odd], o_ix, col_off=i * d_sb, n_cols=d_sb, name=n)
                                            for i, n in enumerate(("q_proj", "k_proj", "v_proj")))
            o_p = sb_prompt(q, k, v, n_batch=bp, t=seq, n_heads=sb_heads)
            outs["sb_kp"].append(k.reshape(bp, seq, sb_heads, HEAD_DIM))
            outs["sb_vp"].append(v.reshape(bp, seq, sb_heads, HEAD_DIM))
            o_s = sb_sample(table, q_s, k_s, v_s, cache_sb_k, cache_sb_v, o_ix, n_batch=bs, t_new=t_new,
                            n_heads=sb_heads, n_pages=n_pages)
            xp, xs = _proj([o_p], [o_s.astype(BF16)], [w_out_odd], o_ix, xp, xs, name="sb_out")
            outs["sb_ks"].append(k_s.reshape(bs, t_new, sb_heads, HEAD_DIM))
            outs["sb_vs"].append(v_s.reshape(bs, t_new, sb_heads, HEAD_DIM))

        gate, up, gate_s, up_s = _proj([rmsnorm_bf16(xp, norm_ffn[layer])], [rmsnorm_bf16(xs, norm_ffn[layer])],
                                       [w_ffn_gate, w_ffn_up], layer, name="ffn_gu")
        act, hf_p = ffn_act(gate, up, zeros_ffn_p, ffn_conv_w[layer], n_batch=bp, t=seq, d_ff=d_ff, out_dtype=BF16)
        act_s, hf_s = ffn_act(gate_s, up_s, state_ffn_conv[layer], ffn_conv_w[layer], n_batch=bs, t=t_new, d_ff=d_ff,
                              out_dtype=F32)
        xp, xs = _proj([act], [act_s.astype(BF16)], [w_down_b], layer, xp, xs, name="ffn_down")
        outs["ffn_p"].append(hf_p)
        outs["ffn_s"].append(hf_s)

    st = lambda key: jnp.stack(outs[key], axis=0)
    return (xp.reshape(bp, seq, d_model), xs.reshape(bs, t_new, d_model), st("conv_p"), st("conv_s"),
            st("cmp_kp"), st("cmp_vp"), st("cmp_ks"), st("cmp_vs"),
            st("sel_kp"), st("sel_vp"), st("sel_ks"), st("sel_vs"),
            st("win_kp"), st("win_vp"), st("win_ks"), st("win_vs"),
            st("sb_kp"), st("sb_vp"), st("sb_ks"), st("sb_vs"),
            st("ffn_p"), st("ffn_s"))
```
